```python
import math
import jax, jax.numpy as jnp
from jax import lax
import numpy as np

D_MODEL = 2048
BATCH = 4
SEQ = 2048
DEPTH = 2
DEC_BATCH = 128
DEC_SEQ = 8
PAST_LEN = 2048
PAGE_SIZE = 128

A_DK = 128
A_DV = 128
A_HEADS = D_MODEL // 4 // A_DV
B_DV = 128
B_DK = 64
B_HEADS = D_MODEL // 4 // B_DV
GLA_LOW_RANK = 16
GLA_TAU = 16.0
C_HD = 128
C_HEADS = D_MODEL // 2 // C_HD
SB_BIAS_INIT = -8.0
MIX_WIDTH = A_HEADS * A_DV + B_HEADS * B_DV + C_HEADS * C_HD
IN_WIDTHS = (A_HEADS * A_DK, A_HEADS * A_DK, A_HEADS * A_DV, A_HEADS * A_DV,
             B_HEADS * B_DK, B_HEADS * B_DK, B_HEADS * B_DV, B_HEADS * B_DV, GLA_LOW_RANK,
             C_HEADS * C_HD, C_HEADS * C_HD, C_HEADS * C_HD)
IN_TOTAL = sum(IN_WIDTHS)
D_FF = 128 * ((8 * D_MODEL // 3 + 127) // 128)
CHUNK = 64
Q_BLOCK = 128
EPS = 1e-6

kernel_name = 'hymba_hgrn2_gla_stickbreaking_macaron_step'


def _rms(x, g):
    xf = x.astype(jnp.float32)
    return xf * lax.rsqrt(jnp.mean(xf * xf, axis=-1, keepdims=True) + EPS) * g.astype(jnp.float32)


def _head_rms(o):
    return o * lax.rsqrt(jnp.mean(o * o, axis=-1, keepdims=True) + EPS)


def _swiglu(h, wg, wu, wd):
    return (jax.nn.silu(h @ wg) * (h @ wu)) @ wd


def _gated_linear_recurrence(q, k, v, log_a, s0):
    f32 = jnp.float32
    B, T, H, dk = q.shape
    dv = v.shape[-1]
    C = math.gcd(T, CHUNK)
    N = T // C

    def chunks(a):
        return a.astype(f32).reshape(B, N, C, H, a.shape[-1]).transpose(1, 0, 3, 2, 4)

    causal = jnp.tril(jnp.ones((C, C), dtype=bool))[None, None, :, :, None]

    def step(S, inp):
        qc, kc, vc, ac = inp
        b = jnp.cumsum(ac, axis=2)
        rel = jnp.where(causal, b[:, :, :, None, :] - b[:, :, None, :, :], -jnp.inf)
        att = jnp.einsum('bhtd,bhtsd,bhsd->bhts', qc, jnp.exp(rel), kc)
        o = (jnp.einsum('bhts,bhsv->bhtv', att, vc)
             + jnp.einsum('bhtd,bhdv->bhtv', qc * jnp.exp(b), S))
        b_end = b[:, :, -1:, :]
        S = (jnp.exp(b_end[:, :, 0, :, None]) * S
             + jnp.einsum('bhsd,bhsv->bhdv', kc * jnp.exp(b_end - b), vc))
        return S, o

    S, o = lax.scan(step, s0.astype(f32), (chunks(q), chunks(k), chunks(v), chunks(log_a)))
    return o.transpose(1, 0, 3, 2, 4).reshape(B, T, H, dv), S


def _stick_breaking(q, k, v, bias, q_pos, k_pos):
    f32 = jnp.float32
    B, T, H, D = q.shape
    qb = math.gcd(T, Q_BLOCK)
    nb = T // qb
    kf = k.astype(f32)
    vf = v.astype(f32)
    bh = bias.astype(f32)[None, :, None, None]

    def block(args):
        qi, pi = args
        z = jnp.einsum('bqhd,bkhd->bhqk', qi.astype(f32), kf) * (D ** -0.5) + bh
        visible = k_pos[None, :] < pi[:, None]
        log_stay = jnp.where(visible, jax.nn.log_sigmoid(-z), 0.0)
        log_rest = lax.cumsum(log_stay, axis=3, reverse=True) - log_stay
        log_w = jnp.where(visible, jax.nn.log_sigmoid(z) + log_rest, -jnp.inf)
        return jnp.einsum('bhqk,bkhd->bqhd', jnp.exp(log_w), vf)

    out = lax.map(block, (q.reshape(B, nb, qb, H, D).swapaxes(0, 1), q_pos.reshape(nb, qb)))
    return out.swapaxes(0, 1).reshape(B, T, H, D)


def _mixer(h, l, P, lb_all, s_hgrn, s_gla, k_past, v_past):
    f32 = jnp.float32
    Bn, T, _ = h.shape
    splits = np.cumsum(IN_WIDTHS)[:-1].tolist()
    (a_q, a_f, a_i, a_g, b_q, b_k, b_v, b_g, b_lr,
     c_q, c_k, c_v) = jnp.split(h @ P['w_in'][l], splits, axis=-1)

    def heads(t, n):
        return t.astype(f32).reshape(Bn, T, n, -1)

    lb = lb_all[l].reshape(A_HEADS, A_DK)
    f_raw = heads(a_f, A_HEADS)
    log_f = jnp.logaddexp(jnp.log(lb), jnp.log1p(-lb) + jax.nn.log_sigmoid(f_raw))
    key_a = (1.0 - lb) * jax.nn.sigmoid(-f_raw)
    o_a, s_hgrn = _gated_linear_recurrence(heads(a_q, A_HEADS), key_a, heads(a_i, A_HEADS), log_f, s_hgrn)
    o_a = (_head_rms(o_a).reshape(Bn, T, -1) * P['onorm_a'][l].astype(f32)
           * jax.nn.sigmoid(a_g.astype(f32)))

    log_alpha = jax.nn.log_sigmoid(b_lr.astype(f32) @ P['w_gla_alpha_up'][l].astype(f32)
                                   + P['b_gla_alpha'][l].astype(f32)) / GLA_TAU
    o_b, s_gla = _gated_linear_recurrence(heads(b_q, B_HEADS) * (B_DK ** -0.5), heads(b_k, B_HEADS),
                                          heads(b_v, B_HEADS), log_alpha.reshape(Bn, T, B_HEADS, B_DK), s_gla)
    o_b = (_head_rms(o_b).reshape(Bn, T, -1) * P['onorm_b'][l].astype(f32)
           * jax.nn.silu(b_g.astype(f32)))

    q_c = _rms(c_q.reshape(Bn, T, C_HEADS, C_HD), P['qnorm_c'][l])
    k_new = _rms(c_k.reshape(Bn, T, C_HEADS, C_HD), P['knorm_c'][l]).astype(h.dtype)
    v_new = c_v.reshape(Bn, T, C_HEADS, C_HD)
    if k_past is None:
        k_all, v_all, pos0 = k_new, v_new, 0
    else:
        k_all = jnp.concatenate([k_past.astype(h.dtype), k_new], axis=1)
        v_all = jnp.concatenate([v_past.astype(h.dtype), v_new], axis=1)
        pos0 = k_past.shape[1]
    q_pos = pos0 + jnp.arange(T, dtype=jnp.int32)
    k_pos = jnp.arange(k_all.shape[1], dtype=jnp.int32)
    o_c = _head_rms(_stick_breaking(q_c, k_all, v_all, P['sb_bias'][l], q_pos, k_pos)).reshape(Bn, T, -1)
    o_c = o_c * P['onorm_c'][l].astype(f32)

    mixed = jnp.concatenate([o_a, o_b, o_c], axis=-1).astype(h.dtype) @ P['w_out'][l]
    return mixed, s_hgrn, s_gla, k_new, v_new


def _layer(x, l, P, lb_all, s_hgrn, s_gla, k_past, v_past):
    x = x + 0.5 * _swiglu(_rms(x, P['norm_ffn1'][l]).astype(x.dtype),
                          P['w_ffn1_gate'][l], P['w_ffn1_up'][l], P['w_ffn1_down'][l])
    m, s_hgrn, s_gla, k_new, v_new = _mixer(_rms(x, P['norm_mix'][l]).astype(x.dtype), l, P, lb_all,
                                            s_hgrn, s_gla, k_past, v_past)
    x = x + m
    x = x + 0.5 * _swiglu(_rms(x, P['norm_ffn2'][l]).astype(x.dtype),
                          P['w_ffn2_gate'][l], P['w_ffn2_up'][l], P['w_ffn2_down'][l])
    return x, s_hgrn, s_gla, k_new, v_new


def setup_inputs(seed: int = 0) -> dict:
    key = jax.random.key(seed)
    ks = iter(jax.random.split(key, 32))
    f32 = jnp.float32
    n_pages = PAST_LEN // PAGE_SIZE
    n_used = DEC_BATCH * n_pages
    n_phys = n_used + max(n_used // 4, 1)

    def dense(shape, fan_in):
        return jax.random.normal(next(ks), shape, f32) * fan_in ** -0.5

    def gain(shape):
        return 1.0 + 0.02 * jax.random.normal(next(ks), shape, f32)

    return {
        'x_prompt': jax.random.normal(next(ks), (BATCH, SEQ, D_MODEL), f32),
        'x_sample': jax.random.normal(next(ks), (DEC_BATCH, DEC_SEQ, D_MODEL), f32),
        'state_hgrn': 0.5 * jax.random.normal(next(ks), (DEPTH, DEC_BATCH, A_HEADS, A_DK, A_DV), f32),
        'state_gla': 0.5 * jax.random.normal(next(ks), (DEPTH, DEC_BATCH, B_HEADS, B_DK, B_DV), f32),
        'cache_k': jax.random.normal(next(ks), (DEPTH, n_phys, PAGE_SIZE, C_HEADS, C_HD), f32),
        'cache_v': jax.random.normal(next(ks), (DEPTH, n_phys, PAGE_SIZE, C_HEADS, C_HD), f32),
        'page_table': jax.random.permutation(next(ks), n_phys)[:n_used].reshape(DEC_BATCH, n_pages).astype(jnp.int32),
        'norm_ffn1': gain((DEPTH, D_MODEL)),
        'w_ffn1_gate': dense((DEPTH, D_MODEL, D_FF), D_MODEL),
        'w_ffn1_up': dense((DEPTH, D_MODEL, D_FF), D_MODEL),
        'w_ffn1_down': dense((DEPTH, D_FF, D_MODEL), D_FF),
        'norm_mix': gain((DEPTH, D_MODEL)),
        'w_in': dense((DEPTH, D_MODEL, IN_TOTAL), D_MODEL),
        'w_gla_alpha_up': dense((DEPTH, GLA_LOW_RANK, B_HEADS * B_DK), GLA_LOW_RANK),
        'b_gla_alpha': 0.1 * jax.random.normal(next(ks), (DEPTH, B_HEADS * B_DK), f32),
        'hgrn_lb_logits': jax.random.normal(next(ks), (DEPTH, A_HEADS * A_DK), f32),
        'qnorm_c': gain((DEPTH, C_HD)),
        'knorm_c': gain((DEPTH, C_HD)),
        'sb_bias': SB_BIAS_INIT + 0.1 * jax.random.normal(next(ks), (DEPTH, C_HEADS), f32),
        'onorm_a': gain((DEPTH, A_HEADS * A_DV)),
        'onorm_b': gain((DEPTH, B_HEADS * B_DV)),
        'onorm_c': gain((DEPTH, C_HEADS * C_HD)),
        'w_out': dense((DEPTH, MIX_WIDTH, D_MODEL), MIX_WIDTH),
        'norm_ffn2': gain((DEPTH, D_MODEL)),
        'w_ffn2_gate': dense((DEPTH, D_MODEL, D_FF), D_MODEL),
        'w_ffn2_up': dense((DEPTH, D_MODEL, D_FF), D_MODEL),
        'w_ffn2_down': dense((DEPTH, D_FF, D_MODEL), D_FF),
    }


def reference(x_prompt, x_sample, state_hgrn, state_gla, cache_k, cache_v, page_table,
              norm_ffn1, w_ffn1_gate, w_ffn1_up, w_ffn1_down, norm_mix, w_in,
              w_gla_alpha_up, b_gla_alpha, hgrn_lb_logits, qnorm_c, knorm_c, sb_bias,
              onorm_a, onorm_b, onorm_c, w_out, norm_ffn2, w_ffn2_gate, w_ffn2_up, w_ffn2_down):
    P = {'norm_ffn1': norm_ffn1, 'w_ffn1_gate': w_ffn1_gate, 'w_ffn1_up': w_ffn1_up,
         'w_ffn1_down': w_ffn1_down, 'norm_mix': norm_mix, 'w_in': w_in,
         'w_gla_alpha_up': w_gla_alpha_up, 'b_gla_alpha': b_gla_alpha,
         'qnorm_c': qnorm_c, 'knorm_c': knorm_c, 'sb_bias': sb_bias,
         'onorm_a': onorm_a, 'onorm_b': onorm_b,
         'onorm_c': onorm_c, 'w_out': w_out, 'norm_ffn2': norm_ffn2, 'w_ffn2_gate': w_ffn2_gate,
         'w_ffn2_up': w_ffn2_up, 'w_ffn2_down': w_ffn2_down}
    lb_cum = jnp.cumsum(jax.nn.softmax(hgrn_lb_logits.astype(jnp.float32), axis=0), axis=0)
    lb_all = lb_cum - lb_cum[0:1]

    dec_b, n_pages = page_table.shape
    past_len = n_pages * cache_k.shape[2]
    bp = x_prompt.shape[0]
    yp, ys = x_prompt, x_sample
    hp_l, gp_l, kp_l, vp_l, hs_l, gs_l, ks_l, vs_l = [], [], [], [], [], [], [], []
    for l in range(DEPTH):
        zeros_h = jnp.zeros((bp, A_HEADS, A_DK, A_DV), jnp.float32)
        zeros_g = jnp.zeros((bp, B_HEADS, B_DK, B_DV), jnp.float32)
        yp, hp, gp, kp, vp = _layer(yp, l, P, lb_all, zeros_h, zeros_g, None, None)
        k_past = cache_k[l][page_table].reshape(dec_b, past_len, C_HEADS, C_HD)
        v_past = cache_v[l][page_table].reshape(dec_b, past_len, C_HEADS, C_HD)
        ys, hs, gs, kn, vn = _layer(ys, l, P, lb_all, state_hgrn[l], state_gla[l], k_past, v_past)
        hp_l.append(hp); gp_l.append(gp); kp_l.append(kp); vp_l.append(vp)
        hs_l.append(hs); gs_l.append(gs); ks_l.append(kn); vs_l.append(vn)
    return (yp, ys, jnp.stack(hp_l), jnp.stack(gp_l), jnp.stack(kp_l), jnp.stack(vp_l),
            jnp.stack(hs_l), jnp.stack(gs_l), jnp.stack(ks_l), jnp.stack(vs_l))
```

```python
import functools
import math

import jax
import jax.numpy as jnp
import numpy as np
from jax import lax
from jax.experimental import pallas as pl
from jax.experimental.pallas import tpu as pltpu

F32 = jnp.float32
BF16 = jnp.bfloat16
EPS = 1e-6

LANES = 128
HEAD = 128
GLA_DK = 64
GLA_TAU = 16.0
N_A = 4
N_B = 4
N_C = 8
PAGE = 128
VMEM_LIMIT = 56 * 1024 * 1024

COL_CQ, COL_CK, COL_CV = 0, 8, 16
COL_AQ, COL_AF, COL_AI, COL_AG = 24, 28, 32, 36
COL_BQ, COL_BK, COL_BV, COL_BG = 40, 44, 48, 52
PROJ_W = 56 * LANES
LR_LANE0 = GLA_DK

FF_TILE = 512
ROW_TILE = 512
PROJ_TILE = 1024
ATT_BLOCK = 256
REC_CHUNK = 64
SAMPLE_GROUP = 16
PAGES_PER_STEP = 4


def _params(sem):
    return pltpu.CompilerParams(dimension_semantics=sem, vmem_limit_bytes=VMEM_LIMIT)


def _dot(a, b):
    return jnp.dot(a, b, preferred_element_type=F32)


def _dot_nt(a, b):
    return lax.dot_general(a, b, (((1,), (1,)), ((), ())), preferred_element_type=F32)


def _dot_tn(a, b):
    return lax.dot_general(a, b, (((0,), (0,)), ((), ())), preferred_element_type=F32)


def _split(x):
    hi = x.astype(BF16)
    lo = (x - hi.astype(F32)).astype(BF16)
    return hi, lo


def _log_sigmoid(x):
    return jnp.minimum(x, 0.0) - jnp.log1p(jnp.exp(-jnp.abs(x)))


def _softplus(x):
    return jnp.maximum(x, 0.0) + jnp.log1p(jnp.exp(-jnp.abs(x)))


def _rms_rows(x):
    return x * lax.rsqrt(jnp.mean(x * x, axis=-1, keepdims=True) + EPS)


def _ffn_kernel(x_ref, g_ref, wg_ref, wu_ref, wd_ref, o_ref, h_ref, acc_ref):
    f = pl.program_id(1)

    @pl.when(f == 0)
    def _():
        h_ref[...] = (_rms_rows(x_ref[...]) * g_ref[...]).astype(BF16)
        acc_ref[...] = jnp.zeros_like(acc_ref)

    h = h_ref[...]
    gate = _dot(h, wg_ref[...])
    up = _dot(h, wu_ref[...])
    act = (gate * jax.nn.sigmoid(gate) * up).astype(BF16)
    acc_ref[...] += _dot(act, wd_ref[...])

    @pl.when(f == pl.num_programs(1) - 1)
    def _():
        o_ref[...] = x_ref[...] + 0.5 * acc_ref[...]


def _ffn(x, g, wg, wu, wd):
    t, d = x.shape
    ffp = wg.shape[1]
    return pl.pallas_call(
        _ffn_kernel,
        grid=(t // ROW_TILE, ffp // FF_TILE),
        in_specs=[
            pl.BlockSpec((ROW_TILE, d), lambda i, f: (i, 0)),
            pl.BlockSpec((1, d), lambda i, f: (0, 0)),
            pl.BlockSpec((d, FF_TILE), lambda i, f: (0, f)),
            pl.BlockSpec((d, FF_TILE), lambda i, f: (0, f)),
            pl.BlockSpec((FF_TILE, d), lambda i, f: (f, 0)),
        ],
        out_specs=pl.BlockSpec((ROW_TILE, d), lambda i, f: (i, 0)),
        out_shape=jax.ShapeDtypeStruct((t, d), F32),
        scratch_shapes=[pltpu.VMEM((ROW_TILE, d), BF16), pltpu.VMEM((ROW_TILE, d), F32)],
        compiler_params=_params(("parallel", "arbitrary")),
        name="ffn",
    )(x, g, wg, wu, wd)


def _proj_kernel(x_ref, g_ref, w_ref, qn_ref, kn_ref, o_ref, h_ref):
    j = pl.program_id(1)

    @pl.when(j == 0)
    def _():
        h_ref[...] = (_rms_rows(x_ref[...]) * g_ref[...]).astype(BF16)

    acc = _dot(h_ref[...], w_ref[...])

    @pl.when(j >= 2)
    def _():
        o_ref[...] = acc

    @pl.when(j < 2)
    def _():
        gain = jnp.where(j == 0, qn_ref[...], kn_ref[...])
        for h in range(PROJ_TILE // HEAD):
            sl = slice(h * HEAD, (h + 1) * HEAD)
            o_ref[:, sl] = _rms_rows(acc[:, sl]) * gain


def _proj(x, g, w, qn, kn):
    t, d = x.shape
    return pl.pallas_call(
        _proj_kernel,
        grid=(t // ROW_TILE, PROJ_W // PROJ_TILE),
        in_specs=[
            pl.BlockSpec((ROW_TILE, d), lambda i, j: (i, 0)),
            pl.BlockSpec((1, d), lambda i, j: (0, 0)),
            pl.BlockSpec((d, PROJ_TILE), lambda i, j: (0, j)),
            pl.BlockSpec((1, HEAD), lambda i, j: (0, 0)),
            pl.BlockSpec((1, HEAD), lambda i, j: (0, 0)),
        ],
        out_specs=pl.BlockSpec((ROW_TILE, PROJ_TILE), lambda i, j: (i, j)),
        out_shape=jax.ShapeDtypeStruct((t, PROJ_W), F32),
        scratch_shapes=[pltpu.VMEM((ROW_TILE, d), BF16)],
        compiler_params=_params(("parallel", "arbitrary")),
        name="proj",
    )(x, g, w, qn, kn)


def _outproj_kernel(x_ref, a_ref, b_ref, c_ref, w_ref, o_ref):
    wa = a_ref.shape[1]
    wb = b_ref.shape[1]
    acc = _dot(a_ref[...].astype(BF16), w_ref[0:wa, :])
    acc += _dot(b_ref[...].astype(BF16), w_ref[wa:wa + wb, :])
    acc += _dot(c_ref[...].astype(BF16), w_ref[wa + wb:, :])
    o_ref[...] = x_ref[...] + acc


def _outproj(x, oa, ob, oc, w):
    t, d = x.shape
    return pl.pallas_call(
        _outproj_kernel,
        grid=(t // ROW_TILE,),
        in_specs=[
            pl.BlockSpec((ROW_TILE, d), lambda i: (i, 0)),
            pl.BlockSpec((ROW_TILE, oa.shape[1]), lambda i: (i, 0)),
            pl.BlockSpec((ROW_TILE, ob.shape[1]), lambda i: (i, 0)),
            pl.BlockSpec((ROW_TILE, oc.shape[1]), lambda i: (i, 0)),
            pl.BlockSpec(w.shape, lambda i: (0, 0)),
        ],
        out_specs=pl.BlockSpec((ROW_TILE, d), lambda i: (i, 0)),
        out_shape=jax.ShapeDtypeStruct((t, d), F32),
        compiler_params=_params(("parallel",)),
        name="outproj",
    )(x, oa, ob, oc, w)


def _rec_consts(chunk, seq):
    levels = int(math.log2(seq))
    t = np.arange(chunk)
    col = t[None, :]
    row = t[:, None]
    w = np.zeros((levels + 2, chunk, chunk), np.float32)
    for l in range(levels):
        m = 1 << l
        same = (row // m) == (col // m)
        right = ((row // m) % 2) == 1
        w[l] = np.where(right, same & (col <= row), same & (col > row))
    same_seq = (row // seq) == (col // seq)
    w[levels] = same_seq & (col <= row)
    w[levels + 1] = same_seq & (col > row)
    lv = np.full((chunk, chunk), -1, np.int32)
    strict = same_seq & (col < row)
    x = np.bitwise_xor(row, col)
    lv[strict] = np.floor(np.log2(np.maximum(x, 1)))[strict].astype(np.int32)
    lv[row == col] = levels
    return jnp.asarray(w.reshape(-1, chunk), BF16), jnp.asarray(lv), levels


def _rec_kernel(*refs, kind, chunk, seq, n_chunks, levels, has_s0, dk):
    refs = list(refs)
    if kind == "hgrn":
        q_ref, f_ref, v_ref, g_ref, lb_ref, on_ref, w_ref, lv_ref = refs[:8]
        rest = refs[8:]
    else:
        q_ref, k_ref, v_ref, g_ref, lr_ref, wup_ref, bup_ref, on_ref, w_ref, lv_ref = refs[:10]
        rest = refs[10:]
    s0_ref = rest.pop(0) if has_s0 else None
    o_ref, s_ref = rest
    n_seq = chunk // seq
    ones_bf = jnp.ones((seq, HEAD), BF16)

    def load_state(n):
        if s0_ref is None:
            return jnp.zeros((HEAD, HEAD), F32)
        s = s0_ref[n, 0]
        if dk < HEAD:
            s = jnp.concatenate([s, jnp.zeros((HEAD - dk, HEAD), F32)], axis=0)
        return s

    def do_chunk(r0, states):
        rows = pl.ds(r0, chunk)
        if kind == "hgrn":
            fr = f_ref[rows, :]
            ls = _log_sigmoid(fr)
            x1 = lb_ref[0:1, :]
            x2 = lb_ref[1:2, :] + ls
            la = jnp.maximum(x1, x2) + jnp.log1p(jnp.exp(-jnp.abs(x1 - x2)))
            k = lb_ref[2:3, :] * jnp.exp(ls - fr)
            q = q_ref[rows, :]
            gate = jax.nn.sigmoid(g_ref[rows, :])
        else:
            x = _dot(lr_ref[rows, :].astype(BF16), wup_ref[...]) + bup_ref[...]
            la = _log_sigmoid(x) * (1.0 / GLA_TAU)
            k = k_ref[rows, :]
            q = q_ref[rows, :] * (GLA_DK ** -0.5)
            g = g_ref[rows, :]
            gate = g * jax.nn.sigmoid(g)
        v = v_ref[rows, :]
        vb = v.astype(BF16)

        hi, lo = _split(la)
        w = w_ref[...]
        ex = jnp.exp(_dot(w, hi) + _dot(w, lo))
        lv = lv_ref[...]
        att = jnp.where(lv == levels, _dot_nt(q.astype(BF16), k.astype(BF16)), 0.0)
        for l in range(levels):
            xl = ex[l * chunk:(l + 1) * chunk]
            p = _dot_nt((q * xl).astype(BF16), (k * xl).astype(BF16))
            att = jnp.where(lv == l, p, att)
        xq = ex[levels * chunk:(levels + 1) * chunk]
        xk = ex[(levels + 1) * chunk:]
        qt = q * xq
        kt = k * xk
        o = _dot(att.astype(BF16), vb)

        new_states = []
        inter = []
        for n in range(n_seq):
            sl = slice(n * seq, (n + 1) * seq)
            s = states[n]
            inter.append(_dot(qt[sl].astype(BF16), s.astype(BF16)))
            h_n, l_n = _split(la[sl])
            decay = jnp.exp(_dot_tn(h_n, ones_bf) + _dot_tn(l_n, ones_bf))
            new_states.append(decay * s + _dot_tn(kt[sl].astype(BF16), vb[sl]))
        o = o + (inter[0] if n_seq == 1 else jnp.concatenate(inter, axis=0))
        o_ref[rows, :] = (_rms_rows(o) * on_ref[...] * gate).astype(o_ref.dtype)
        return tuple(new_states)

    states = tuple(load_state(n) for n in range(n_seq))
    if n_chunks == 1:
        states = do_chunk(0, states)
    else:
        states = lax.fori_loop(
            0, n_chunks, lambda c, st: do_chunk(pl.multiple_of(c * chunk, chunk), st), states)
    for n in range(n_seq):
        s_ref[n, 0] = states[n][:dk]


def _recurrence(kind, proj, row_block0, n_groups, s0, consts, layer_consts, *, chunk, seq, n_chunks, out_dtype):
    w, lv, levels = consts
    rows = chunk * n_chunks
    n_seq = chunk // seq
    dk = HEAD if kind == "hgrn" else GLA_DK
    n_heads = N_A if kind == "hgrn" else N_B

    def col(c0):
        return pl.BlockSpec((rows, HEAD), lambda g, h: (row_block0 + g, c0 + h))

    def per_head(nrows):
        return pl.BlockSpec((nrows, HEAD), lambda g, h: (0, h))

    def whole(a):
        return pl.BlockSpec(a.shape, lambda g, h: (0, 0))

    if kind == "hgrn":
        lb3, onorm = layer_consts
        args = [proj, proj, proj, proj, lb3, onorm, w, lv]
        specs = [col(COL_AQ), col(COL_AF), col(COL_AI), col(COL_AG), per_head(3), per_head(1), whole(w), whole(lv)]
    else:
        wup, bup, onorm = layer_consts
        args = [proj, proj, proj, proj, proj, wup, bup, onorm, w, lv]
        specs = [col(COL_BQ), col(COL_BK), col(COL_BV), col(COL_BG),
                 pl.BlockSpec((rows, HEAD), lambda g, h: (row_block0 + g, COL_BQ)),
                 pl.BlockSpec((HEAD, HEAD), lambda g, h: (0, h)), per_head(1), per_head(1), whole(w), whole(lv)]
    state_spec = pl.BlockSpec((n_seq, 1, dk, HEAD), lambda g, h: (g, h, 0, 0))
    if s0 is not None:
        args.append(s0)
        specs.append(state_spec)
    kern = functools.partial(_rec_kernel, kind=kind, chunk=chunk, seq=seq, n_chunks=n_chunks,
                             levels=levels, has_s0=s0 is not None, dk=dk)
    return pl.pallas_call(
        kern,
        grid=(n_groups, n_heads),
        in_specs=specs,
        out_specs=[pl.BlockSpec((rows, HEAD), lambda g, h: (g, h)), state_spec],
        out_shape=[jax.ShapeDtypeStruct((n_groups * rows, n_heads * HEAD), out_dtype),
                   jax.ShapeDtypeStruct((n_groups * n_seq, n_heads, dk, HEAD), F32)],
        compiler_params=_params(("parallel", "parallel")),
        name="rec_" + kind,
    )(*args)


def _tri_suffix(n):
    j = np.arange(n)
    return jnp.asarray(j[:, None] >= j[None, :], BF16)


def _sb_prompt_kernel(bias_ref, q_ref, k_ref, v_ref, on_ref, tri_ref, o_ref, *, blk, scale):
    h = pl.program_id(1)
    qb = pl.program_id(2)
    bias = bias_ref[h]
    q = q_ref[...].astype(BF16)
    tri = tri_ref[...]

    def block(kb, masked, acc, run):
        rows = pl.ds(pl.multiple_of(kb * blk, blk), blk)
        z = _dot_nt(q, k_ref[rows, :].astype(BF16)) * scale + bias
        sp = _softplus(z)
        if masked:
            vis = lax.broadcasted_iota(jnp.int32, (blk, blk), 1) < lax.broadcasted_iota(jnp.int32, (blk, blk), 0)
            sp = jnp.where(vis, sp, 0.0)
        hi, lo = _split(sp)
        cs = _dot(hi, tri) + _dot(lo, tri)
        w = jnp.exp(z - (cs + run))
        if masked:
            w = jnp.where(vis, w, 0.0)
        acc = acc + _dot(w.astype(BF16), v_ref[rows, :].astype(BF16))
        return acc, run + cs[:, 0:1]

    acc, run = block(qb, True, jnp.zeros((blk, HEAD), F32), jnp.zeros((blk, 1), F32))
    acc, run = lax.fori_loop(1, qb + 1, lambda j, c: block(qb - j, False, *c), (acc, run))
    o_ref[...] = (_rms_rows(acc) * on_ref[...]).astype(o_ref.dtype)


def _sb_prompt(proj, bias, onorm, n_batch, t):
    blk = ATT_BLOCK
    nq = t // blk
    tri = _tri_suffix(blk)
    kern = functools.partial(_sb_prompt_kernel, blk=blk, scale=HEAD ** -0.5)
    return pl.pallas_call(
        kern,
        grid=(n_batch, N_C, nq),
        in_specs=[
            pl.BlockSpec(memory_space=pltpu.SMEM),
            pl.BlockSpec((blk, HEAD), lambda b, h, i: (b * nq + i, COL_CQ + h)),
            pl.BlockSpec((t, HEAD), lambda b, h, i: (b, COL_CK + h)),
            pl.BlockSpec((t, HEAD), lambda b, h, i: (b, COL_CV + h)),
            pl.BlockSpec((1, HEAD), lambda b, h, i: (0, h)),
            pl.BlockSpec((blk, blk), lambda b, h, i: (0, 0)),
        ],
        out_specs=pl.BlockSpec((blk, HEAD), lambda b, h, i: (b * nq + i, h)),
        out_shape=jax.ShapeDtypeStruct((n_batch * t, N_C * HEAD), BF16),
        compiler_params=_params(("parallel", "parallel", "arbitrary")),
        name="sb_prompt",
    )(bias, proj, proj, proj, onorm, tri)


def _sb_sample_kernel(pt_ref, *refs, n_new, scale):
    del pt_ref
    npg = PAGES_PER_STEP
    q_ref, kn_ref, vn_ref, bias_ref, on_ref, tri_ref = refs[:6]
    k_refs = refs[6:6 + npg]
    v_refs = refs[6 + npg:6 + 2 * npg]
    o_ref, qm_ref, acc_ref, run_ref = refs[6 + 2 * npg:]
    j = pl.program_id(1)
    width = N_C * HEAD
    ncol = N_C * n_new

    @pl.when(j == 0)
    def _():
        q = q_ref[...]
        qrep = jnp.concatenate([q] * N_C, axis=0)
        head_of_row = lax.broadcasted_iota(jnp.int32, (ncol, width), 0) // n_new
        head_of_lane = lax.broadcasted_iota(jnp.int32, (ncol, width), 1) // HEAD
        qm_ref[...] = jnp.where(head_of_row == head_of_lane, qrep, 0.0).astype(BF16)
        z = _dot_nt(kn_ref[...].astype(BF16), qm_ref[...]) * scale + bias_ref[...]
        s_idx = lax.broadcasted_iota(jnp.int32, (n_new, ncol), 0)
        i_idx = lax.broadcasted_iota(jnp.int32, (n_new, ncol), 1) % n_new
        vis = s_idx < i_idx
        sp = jnp.where(vis, _softplus(z), 0.0)
        suffix = [None] * n_new
        tot = jnp.zeros((1, ncol), F32)
        for s in range(n_new - 1, -1, -1):
            tot = tot + sp[s:s + 1, :]
            suffix[s] = tot
        cs = jnp.concatenate(suffix, axis=0)
        w = jnp.where(vis, jnp.exp(z - cs), 0.0)
        acc_ref[...] = _dot_tn(w.astype(BF16), vn_ref[...].astype(BF16))
        run_ref[...] = jnp.broadcast_to(tot, run_ref.shape)

    qm = qm_ref[...]
    tri = tri_ref[...]
    run = run_ref[0:1, :]
    acc = acc_ref[...]
    for p in range(npg):
        z = _dot_nt(k_refs[p][...].astype(BF16), qm) * scale + bias_ref[...]
        sp = _softplus(z)
        hi, lo = _split(sp)
        cs = _dot(tri, hi) + _dot(tri, lo)
        w = jnp.exp(z - (cs + run))
        acc = acc + _dot_tn(w.astype(BF16), v_refs[p][...].astype(BF16))
        run = run + cs[0:1, :]
    acc_ref[...] = acc
    run_ref[...] = jnp.broadcast_to(run, run_ref.shape)

    @pl.when(j == pl.num_programs(1) - 1)
    def _():
        for h in range(N_C):
            o = acc_ref[h * n_new:(h + 1) * n_new, h * HEAD:(h + 1) * HEAD]
            o_ref[:, h * HEAD:(h + 1) * HEAD] = _rms_rows(o) * on_ref[:, h * HEAD:(h + 1) * HEAD]


def _sb_sample(proj, cache_k, cache_v, page_idx, bias_cols, onorm, n_batch, n_new, n_pages):
    npg = PAGES_PER_STEP
    steps = n_pages // npg
    width = N_C * HEAD
    ncol = N_C * n_new
    tri = jnp.asarray(np.arange(PAGE)[None, :] >= np.arange(PAGE)[:, None], BF16)

    def page_spec(p):
        return pl.BlockSpec((None, PAGE, width),
                            lambda b, j, pt: (pt[b * n_pages + n_pages - 1 - (j * npg + p)], 0, 0))

    def const(shape):
        return pl.BlockSpec(shape, lambda b, j, pt: (0, 0))

    grid_spec = pltpu.PrefetchScalarGridSpec(
        num_scalar_prefetch=1,
        grid=(n_batch, steps),
        in_specs=[
            pl.BlockSpec((n_new, width), lambda b, j, pt: (b, COL_CQ // N_C)),
            pl.BlockSpec((n_new, width), lambda b, j, pt: (b, COL_CK // N_C)),
            pl.BlockSpec((n_new, width), lambda b, j, pt: (b, COL_CV // N_C)),
            const((1, ncol)), const((1, width)), const((PAGE, PAGE)),
        ] + [page_spec(p) for p in range(npg)] * 2,
        out_specs=pl.BlockSpec((n_new, width), lambda b, j, pt: (b, 0)),
        scratch_shapes=[pltpu.VMEM((ncol, width), BF16), pltpu.VMEM((ncol, width), F32),
                        pltpu.VMEM((8, ncol), F32)],
    )
    kern = functools.partial(_sb_sample_kernel, n_new=n_new, scale=HEAD ** -0.5)
    return pl.pallas_call(
        kern,
        grid_spec=grid_spec,
        out_shape=jax.ShapeDtypeStruct((n_batch * n_new, width), F32),
        compiler_params=_params(("parallel", "arbitrary")),
        name="sb_sample",
    )(page_idx, proj, proj, proj, bias_cols, onorm, tri, *([cache_k] * npg), *([cache_v] * npg))


def _prep_w_in(w):
    d = w.shape[0]
    a, b_q, b_k, b_v, b_g, b_lr, c = (w[:, :2048], w[:, 2048:2304], w[:, 2304:2560], w[:, 2560:3072],
                                      w[:, 3072:3584], w[:, 3584:3600], w[:, 3600:])
    zq = jnp.zeros((d, HEAD - GLA_DK), w.dtype)
    bq_parts, bk_parts = [], []
    for h in range(N_B):
        sl = slice(h * GLA_DK, (h + 1) * GLA_DK)
        pad = jnp.concatenate([b_lr, zq[:, b_lr.shape[1]:]], axis=1) if h == 0 else zq
        bq_parts += [b_q[:, sl], pad]
        bk_parts += [b_k[:, sl], zq]
    out = jnp.concatenate([c, a] + bq_parts + bk_parts + [b_v, b_g], axis=1)
    assert out.shape[1] == PROJ_W
    return out.astype(BF16)


def _prep_gla_gate(w_up, b_up):
    rank = w_up.shape[0]
    wp = jnp.zeros((HEAD, N_B * HEAD), F32)
    bp = jnp.zeros((1, N_B * HEAD), F32)
    for h in range(N_B):
        wp = wp.at[LR_LANE0:LR_LANE0 + rank, h * HEAD:h * HEAD + GLA_DK].set(w_up[:, h * GLA_DK:(h + 1) * GLA_DK])
        bp = bp.at[0, h * HEAD:h * HEAD + GLA_DK].set(b_up[h * GLA_DK:(h + 1) * GLA_DK])
    return wp.astype(BF16), bp


def _pad_ff(w, axis):
    ff = w.shape[axis]
    ffp = -(-ff // FF_TILE) * FF_TILE
    pad = [(0, 0), (0, 0)]
    pad[axis] = (0, ffp - ff)
    return jnp.pad(w, pad).astype(BF16)


def kernel(x_prompt, x_sample, state_hgrn, state_gla, cache_k, cache_v, page_table, norm_ffn1, w_ffn1_gate, w_ffn1_up, w_ffn1_down, norm_mix, w_in, w_gla_alpha_up, b_gla_alpha, hgrn_lb_logits, qnorm_c, knorm_c, sb_bias, onorm_a, onorm_b, onorm_c, w_out, norm_ffn2, w_ffn2_gate, w_ffn2_up, w_ffn2_down):
    depth = w_in.shape[0]
    bp, tp, d = x_prompt.shape
    bs, ts, _ = x_sample.shape
    n_pages = page_table.shape[1]
    n_phys = cache_k.shape[1]
    width = N_C * HEAD

    lb_cum = jnp.cumsum(jax.nn.softmax(hgrn_lb_logits.astype(F32), axis=0), axis=0)
    lb_all = lb_cum - lb_cum[0:1]

    xp = x_prompt.reshape(bp * tp, d)
    xs = x_sample.reshape(bs * ts, d)
    ck = cache_k.reshape(depth * n_phys, PAGE, width)
    cv = cache_v.reshape(depth * n_phys, PAGE, width)
    pt_flat = page_table.reshape(-1).astype(jnp.int32)

    consts_p = _rec_consts(REC_CHUNK, REC_CHUNK)
    consts_s = _rec_consts(SAMPLE_GROUP * ts, ts)

    outs = {k: [] for k in ("hp", "gp", "kp", "vp", "hs", "gs", "ks", "vs")}
    for l in range(depth):
        row = lambda a: a[l].reshape(1, -1).astype(F32)
        ffn1 = (row(norm_ffn1), _pad_ff(w_ffn1_gate[l], 1), _pad_ff(w_ffn1_up[l], 1), _pad_ff(w_ffn1_down[l], 0))
        ffn2 = (row(norm_ffn2), _pad_ff(w_ffn2_gate[l], 1), _pad_ff(w_ffn2_up[l], 1), _pad_ff(w_ffn2_down[l], 0))
        w_in_l = _prep_w_in(w_in[l])
        w_out_l = w_out[l].astype(BF16)
        lb = lb_all[l]
        lb3 = jnp.stack([jnp.log(lb), jnp.log1p(-lb), 1.0 - lb])
        hgrn_consts = (lb3, row(onorm_a))
        gla_consts = _prep_gla_gate(w_gla_alpha_up[l].astype(F32), b_gla_alpha[l].astype(F32)) + (row(onorm_b),)
        bias = sb_bias[l].astype(F32)
        on_c = row(onorm_c)

        xp = _ffn(xp, *ffn1)
        pj = _proj(xp, row(norm_mix), w_in_l, row(qnorm_c), row(knorm_c))
        rec = dict(chunk=REC_CHUNK, seq=REC_CHUNK, n_chunks=tp // REC_CHUNK, out_dtype=BF16)
        oa, hp = _recurrence("hgrn", pj, 0, bp, None, consts_p, hgrn_consts, **rec)
        ob, gp = _recurrence("gla", pj, 0, bp, None, consts_p, gla_consts, **rec)
        oc = _sb_prompt(pj, bias, on_c, bp, tp)
        xp = _outproj(xp, oa, ob, oc, w_out_l)
        xp = _ffn(xp, *ffn2)
        outs["hp"].append(hp)
        outs["gp"].append(gp)
        outs["kp"].append(pj[:, COL_CK * LANES:COL_CK * LANES + width].reshape(bp, tp, N_C, HEAD))
        outs["vp"].append(pj[:, COL_CV * LANES:COL_CV * LANES + width].reshape(bp, tp, N_C, HEAD))

        xs = _ffn(xs, *ffn1)
        pj = _proj(xs, row(norm_mix), w_in_l, row(qnorm_c), row(knorm_c))
        rec = dict(chunk=SAMPLE_GROUP * ts, seq=ts, n_chunks=1, out_dtype=BF16)
        oa, hs = _recurrence("hgrn", pj, 0, bs // SAMPLE_GROUP, state_hgrn[l], consts_s, hgrn_consts, **rec)
        ob, gs = _recurrence("gla", pj, 0, bs // SAMPLE_GROUP, state_gla[l], consts_s, gla_consts, **rec)
        oc = _sb_sample(pj, ck, cv, pt_flat + l * n_phys, jnp.repeat(bias, ts).reshape(1, -1), on_c, bs, ts, n_pages)
        xs = _outproj(xs, oa, ob, oc, w_out_l)
        xs = _ffn(xs, *ffn2)
        outs["hs"].append(hs)
        outs["gs"].append(gs)
        outs["ks"].append(pj[:, COL_CK * LANES:COL_CK * LANES + width].reshape(bs, ts, N_C, HEAD))
        outs["vs"].append(pj[:, COL_CV * LANES:COL_CV * LANES + width].reshape(bs, ts, N_C, HEAD))

    st = lambda k: jnp.stack(outs[k])
    return (xp.reshape(bp, tp, d), xs.reshape(bs, ts, d), st("hp"), st("gp"), st("kp"), st("vp"),
            st("hs"), st("gs"), st("ks"), st("vs"))
```

```python
import functools
import math

import jax
import jax.numpy as jnp
import numpy as np
from jax import lax
from jax.experimental import pallas as pl
from jax.experimental.pallas import tpu as pltpu

F32 = jnp.float32
BF16 = jnp.bfloat16
EPS = 1e-6

LANES = 128
HEAD = 128
GLA_DK = 64
GLA_TAU = 16.0
N_A = 4
N_B = 4
N_C = 8
PAGE = 128
VMEM_LIMIT = 56 * 1024 * 1024

COL_CQ, COL_CK, COL_CV = 0, 8, 16
COL_AQ, COL_AF, COL_AI, COL_AG = 24, 28, 32, 36
COL_BQ, COL_BK, COL_BV, COL_BG = 40, 44, 48, 52
PROJ_W = 56 * LANES
LR_LANE0 = GLA_DK

FF_TILE = 512
ROW_TILE = 512
PROJ_TILE = 1024
ATT_BLOCK = 256
ATT_HEADS_PER_STEP = 2
REC_CHUNK = 64
REC_ROWS = 512
SAMPLE_GROUP = 16
PAGES_PER_STEP = 8


def _params(sem):
    return pltpu.CompilerParams(dimension_semantics=sem, vmem_limit_bytes=VMEM_LIMIT)


def _dot(a, b):
    return jnp.dot(a, b, preferred_element_type=F32)


def _dot_nt(a, b):
    return lax.dot_general(a, b, (((1,), (1,)), ((), ())), preferred_element_type=F32)


def _dot_tn(a, b):
    return lax.dot_general(a, b, (((0,), (0,)), ((), ())), preferred_element_type=F32)


def _split(x):
    hi = x.astype(BF16)
    lo = (x - hi.astype(F32)).astype(BF16)
    return hi, lo


def _log_sigmoid(x):
    return jnp.minimum(x, 0.0) - jnp.log1p(jnp.exp(-jnp.abs(x)))


def _softplus(x):
    return jnp.maximum(x, 0.0) + jnp.log1p(jnp.exp(-jnp.abs(x)))


def _rms_rows(x):
    return x * lax.rsqrt(jnp.mean(x * x, axis=-1, keepdims=True) + EPS)


def _ffn_kernel(x_ref, g_ref, wg_ref, wu_ref, wd_ref, o_ref, h_ref, acc_ref):
    f = pl.program_id(1)

    @pl.when(f == 0)
    def _():
        h_ref[...] = (_rms_rows(x_ref[...]) * g_ref[...]).astype(BF16)
        acc_ref[...] = jnp.zeros_like(acc_ref)

    h = h_ref[...]
    gate = _dot(h, wg_ref[...])
    up = _dot(h, wu_ref[...])
    act = (gate * jax.nn.sigmoid(gate) * up).astype(BF16)
    acc_ref[...] += _dot(act, wd_ref[...])

    @pl.when(f == pl.num_programs(1) - 1)
    def _():
        o_ref[...] = x_ref[...] + 0.5 * acc_ref[...]


def _ffn(x, g, wg, wu, wd):
    t, d = x.shape
    ffp = wg.shape[1]
    return pl.pallas_call(
        _ffn_kernel,
        grid=(t // ROW_TILE, ffp // FF_TILE),
        in_specs=[
            pl.BlockSpec((ROW_TILE, d), lambda i, f: (i, 0)),
            pl.BlockSpec((1, d), lambda i, f: (0, 0)),
            pl.BlockSpec((d, FF_TILE), lambda i, f: (0, f)),
            pl.BlockSpec((d, FF_TILE), lambda i, f: (0, f)),
            pl.BlockSpec((FF_TILE, d), lambda i, f: (f, 0)),
        ],
        out_specs=pl.BlockSpec((ROW_TILE, d), lambda i, f: (i, 0)),
        out_shape=jax.ShapeDtypeStruct((t, d), F32),
        scratch_shapes=[pltpu.VMEM((ROW_TILE, d), BF16), pltpu.VMEM((ROW_TILE, d), F32)],
        compiler_params=_params(("parallel", "arbitrary")),
        name="ffn",
    )(x, g, wg, wu, wd)


def _proj_kernel(x_ref, g_ref, w_ref, qn_ref, kn_ref, o_ref, h_ref):
    j = pl.program_id(1)

    @pl.when(j == 0)
    def _():
        h_ref[...] = (_rms_rows(x_ref[...]) * g_ref[...]).astype(BF16)

    acc = _dot(h_ref[...], w_ref[...])

    @pl.when(j >= 2)
    def _():
        o_ref[...] = acc

    @pl.when(j < 2)
    def _():
        gain = jnp.where(j == 0, qn_ref[...], kn_ref[...])
        for h in range(PROJ_TILE // HEAD):
            sl = slice(h * HEAD, (h + 1) * HEAD)
            o_ref[:, sl] = _rms_rows(acc[:, sl]) * gain


def _proj(x, g, w, qn, kn):
    t, d = x.shape
    return pl.pallas_call(
        _proj_kernel,
        grid=(t // ROW_TILE, PROJ_W // PROJ_TILE),
        in_specs=[
            pl.BlockSpec((ROW_TILE, d), lambda i, j: (i, 0)),
            pl.BlockSpec((1, d), lambda i, j: (0, 0)),
            pl.BlockSpec((d, PROJ_TILE), lambda i, j: (0, j)),
            pl.BlockSpec((1, HEAD), lambda i, j: (0, 0)),
            pl.BlockSpec((1, HEAD), lambda i, j: (0, 0)),
        ],
        out_specs=pl.BlockSpec((ROW_TILE, PROJ_TILE), lambda i, j: (i, j)),
        out_shape=jax.ShapeDtypeStruct((t, PROJ_W), F32),
        scratch_shapes=[pltpu.VMEM((ROW_TILE, d), BF16)],
        compiler_params=_params(("parallel", "arbitrary")),
        name="proj",
    )(x, g, w, qn, kn)


def _outproj_kernel(x_ref, a_ref, b_ref, c_ref, w_ref, o_ref):
    wa = a_ref.shape[1]
    wb = b_ref.shape[1]
    acc = _dot(a_ref[...].astype(BF16), w_ref[0:wa, :])
    acc += _dot(b_ref[...].astype(BF16), w_ref[wa:wa + wb, :])
    acc += _dot(c_ref[...].astype(BF16), w_ref[wa + wb:, :])
    o_ref[...] = x_ref[...] + acc


def _outproj(x, oa, ob, oc, w):
    t, d = x.shape
    return pl.pallas_call(
        _outproj_kernel,
        grid=(t // ROW_TILE,),
        in_specs=[
            pl.BlockSpec((ROW_TILE, d), lambda i: (i, 0)),
            pl.BlockSpec((ROW_TILE, oa.shape[1]), lambda i: (i, 0)),
            pl.BlockSpec((ROW_TILE, ob.shape[1]), lambda i: (i, 0)),
            pl.BlockSpec((ROW_TILE, oc.shape[1]), lambda i: (i, 0)),
            pl.BlockSpec(w.shape, lambda i: (0, 0)),
        ],
        out_specs=pl.BlockSpec((ROW_TILE, d), lambda i: (i, 0)),
        out_shape=jax.ShapeDtypeStruct((t, d), F32),
        compiler_params=_params(("parallel",)),
        name="outproj",
    )(x, oa, ob, oc, w)


def _rec_consts(chunk, seq):
    levels = int(math.log2(seq))
    t = np.arange(chunk)
    col = t[None, :]
    row = t[:, None]
    w = np.zeros((levels + 2, chunk, chunk), np.float32)
    for l in range(levels):
        m = 1 << l
        same = (row // m) == (col // m)
        right = ((row // m) % 2) == 1
        w[l] = np.where(right, same & (col <= row), same & (col > row))
    same_seq = (row // seq) == (col // seq)
    w[levels] = same_seq & (col <= row)
    w[levels + 1] = same_seq & (col > row)
    lv = np.full((chunk, chunk), -1, np.int32)
    strict = same_seq & (col < row)
    x = np.bitwise_xor(row, col)
    lv[strict] = np.floor(np.log2(np.maximum(x, 1)))[strict].astype(np.int32)
    lv[row == col] = levels
    return jnp.asarray(w.reshape(-1, chunk), BF16), jnp.asarray(lv), levels


def _rec_kernel(*refs, kind, chunk, seq, n_chunks, levels, has_s0, dk, n_heads, carry):
    refs = list(refs)
    if kind == "hgrn":
        q_ref, f_ref, v_ref, g_ref, lb_ref, on_ref, w_ref, lv_ref = refs[:8]
        rest = refs[8:]
    else:
        q_ref, k_ref, v_ref, g_ref, wup_ref, bup_ref, on_ref, w_ref, lv_ref = refs[:9]
        rest = refs[9:]
    s0_ref = rest.pop(0) if has_s0 else None
    o_ref, s_ref = rest[:2]
    st_ref = rest[2] if carry else None
    n_seq = chunk // seq
    step = pl.program_id(1)
    ones_bf = jnp.ones((seq, HEAD), BF16)

    def chunk_head(rows, hh, states):
        cols = slice(hh * HEAD, (hh + 1) * HEAD)
        if kind == "hgrn":
            fr = f_ref[rows, cols]
            ls = _log_sigmoid(fr)
            x1 = lb_ref[0:1, cols]
            x2 = lb_ref[1:2, cols] + ls
            la = jnp.maximum(x1, x2) + jnp.log1p(jnp.exp(-jnp.abs(x1 - x2)))
            k = lb_ref[2:3, cols] * jnp.exp(ls - fr)
            q = q_ref[rows, cols]
            gate = jax.nn.sigmoid(g_ref[rows, cols])
        else:
            x = _dot(q_ref[rows, 0:HEAD].astype(BF16), wup_ref[:, cols]) + bup_ref[:, cols]
            la = _log_sigmoid(x) * (1.0 / GLA_TAU)
            k = k_ref[rows, cols]
            q = q_ref[rows, cols] * (GLA_DK ** -0.5)
            g = g_ref[rows, cols]
            gate = g * jax.nn.sigmoid(g)
        vb = v_ref[rows, cols].astype(BF16)

        hi, lo = _split(la)
        w = w_ref[...]
        ex = jnp.exp(_dot(w, hi) + _dot(w, lo))
        lv = lv_ref[...]
        att = jnp.where(lv == levels, _dot_nt(q.astype(BF16), k.astype(BF16)), 0.0)
        for l in range(levels):
            xl = ex[l * chunk:(l + 1) * chunk]
            p = _dot_nt((q * xl).astype(BF16), (k * xl).astype(BF16))
            att = jnp.where(lv == l, p, att)
        xq = ex[levels * chunk:(levels + 1) * chunk]
        xk = ex[(levels + 1) * chunk:]
        qt = q * xq
        kt = k * xk
        o = _dot(att.astype(BF16), vb)

        new_states = []
        inter = []
        for n in range(n_seq):
            sl = slice(n * seq, (n + 1) * seq)
            s = states[n]
            inter.append(_dot(qt[sl].astype(BF16), s.astype(BF16)))
            h_n, l_n = _split(la[sl])
            decay = jnp.exp(_dot_tn(h_n, ones_bf) + _dot_tn(l_n, ones_bf))
            new_states.append(decay * s + _dot_tn(kt[sl].astype(BF16), vb[sl]))
        o = o + (inter[0] if n_seq == 1 else jnp.concatenate(inter, axis=0))
        o_ref[rows, cols] = (_rms_rows(o) * on_ref[:, cols] * gate).astype(o_ref.dtype)
        return new_states

    def do_chunk(r0, states):
        rows = pl.ds(r0, chunk)
        out = []
        for hh in range(n_heads):
            out += chunk_head(rows, hh, states[hh * n_seq:(hh + 1) * n_seq])
        return tuple(out)

    def initial(hh, n):
        if s0_ref is None:
            return jnp.zeros((HEAD, HEAD), F32)
        s = s0_ref[n, hh]
        if dk < HEAD:
            s = jnp.concatenate([s, jnp.zeros((HEAD - dk, HEAD), F32)], axis=0)
        return s

    if carry:
        @pl.when(step == 0)
        def _():
            for hh in range(n_heads):
                st_ref[hh] = initial(hh, 0)
        states = tuple(st_ref[hh] for hh in range(n_heads))
    else:
        states = tuple(initial(hh, n) for hh in range(n_heads) for n in range(n_seq))
    if n_chunks == 1:
        states = do_chunk(0, states)
    else:
        states = lax.fori_loop(
            0, n_chunks, lambda c, st: do_chunk(pl.multiple_of(c * chunk, chunk), st), states)
    if carry:
        for hh in range(n_heads):
            st_ref[hh] = states[hh]

        @pl.when(step == pl.num_programs(1) - 1)
        def _():
            for hh in range(n_heads):
                s_ref[0, hh] = states[hh][:dk]
    else:
        for hh in range(n_heads):
            for n in range(n_seq):
                s_ref[n, hh] = states[hh * n_seq + n][:dk]


def _recurrence(kind, proj, n_groups, s0, consts, layer_consts, *, chunk, seq, n_chunks, seq_steps, out_dtype):
    w, lv, levels = consts
    rows = chunk * n_chunks
    n_seq = chunk // seq
    dk = HEAD if kind == "hgrn" else GLA_DK
    n_heads = N_A if kind == "hgrn" else N_B
    carry = seq_steps > 1
    assert n_seq == 1 or not carry
    assert s0 is None or not carry

    def col(c0):
        return pl.BlockSpec((rows, n_heads * HEAD), lambda g, t: (g * seq_steps + t, c0 // n_heads))

    def whole(a):
        return pl.BlockSpec(a.shape, lambda g, t: (0,) * a.ndim)

    if kind == "hgrn":
        args = [proj, proj, proj, proj, *layer_consts, w, lv]
        specs = [col(COL_AQ), col(COL_AF), col(COL_AI), col(COL_AG)]
    else:
        args = [proj, proj, proj, proj, *layer_consts, w, lv]
        specs = [col(COL_BQ), col(COL_BK), col(COL_BV), col(COL_BG)]
    specs += [whole(a) for a in args[4:]]
    state_spec = pl.BlockSpec((n_seq, n_heads, dk, HEAD), lambda g, t: (g, 0, 0, 0))
    if s0 is not None:
        args.append(s0)
        specs.append(state_spec)
    kern = functools.partial(_rec_kernel, kind=kind, chunk=chunk, seq=seq, n_chunks=n_chunks, levels=levels,
                             has_s0=s0 is not None, dk=dk, n_heads=n_heads, carry=carry)
    return pl.pallas_call(
        kern,
        grid=(n_groups, seq_steps),
        in_specs=specs,
        out_specs=[pl.BlockSpec((rows, n_heads * HEAD), lambda g, t: (g * seq_steps + t, 0)), state_spec],
        out_shape=[jax.ShapeDtypeStruct((n_groups * seq_steps * rows, n_heads * HEAD), out_dtype),
                   jax.ShapeDtypeStruct((n_groups * n_seq, n_heads, dk, HEAD), F32)],
        scratch_shapes=[pltpu.VMEM((n_heads, HEAD, HEAD), F32)] if carry else [],
        compiler_params=_params(("parallel", "arbitrary")),
        name="rec_" + kind,
    )(*args)


def _tri_suffix(n):
    j = np.arange(n)
    return jnp.asarray(j[:, None] >= j[None, :], BF16)


def _sb_prompt_kernel(bias_ref, q_ref, k_ref, v_ref, on_ref, tri_ref, o_ref, *, blk, scale, n_heads):
    h0 = pl.program_id(1) * n_heads
    qb = pl.program_id(2)
    tri = tri_ref[...]
    cols = [slice(i * HEAD, (i + 1) * HEAD) for i in range(n_heads)]
    q = [q_ref[:, c].astype(BF16) for c in cols]
    bias = [bias_ref[h0 + i] for i in range(n_heads)]

    def block(kb, masked, carry):
        rows = pl.ds(pl.multiple_of(kb * blk, blk), blk)
        out = []
        for i in range(n_heads):
            acc, run = carry[i]
            z = _dot_nt(q[i], k_ref[rows, cols[i]].astype(BF16)) * scale + bias[i]
            sp = _softplus(z)
            if masked:
                vis = lax.broadcasted_iota(jnp.int32, (blk, blk), 1) < lax.broadcasted_iota(jnp.int32, (blk, blk), 0)
                sp = jnp.where(vis, sp, 0.0)
            hi, lo = _split(sp)
            cs = _dot(hi, tri) + _dot(lo, tri)
            w = jnp.exp(z - (cs + run))
            if masked:
                w = jnp.where(vis, w, 0.0)
            acc = acc + _dot(w.astype(BF16), v_ref[rows, cols[i]].astype(BF16))
            out.append((acc, run + cs[:, 0:1]))
        return tuple(out)

    zero = (jnp.zeros((blk, HEAD), F32), jnp.zeros((blk, 1), F32))
    carry = block(qb, True, (zero,) * n_heads)
    carry = lax.fori_loop(1, qb + 1, lambda j, c: block(qb - j, False, c), carry)
    for i in range(n_heads):
        o_ref[:, cols[i]] = (_rms_rows(carry[i][0]) * on_ref[:, cols[i]]).astype(o_ref.dtype)


def _sb_prompt(proj, bias, onorm, n_batch, t):
    blk = ATT_BLOCK
    nq = t // blk
    nh = ATT_HEADS_PER_STEP
    tri = _tri_suffix(blk)
    kern = functools.partial(_sb_prompt_kernel, blk=blk, scale=HEAD ** -0.5, n_heads=nh)
    return pl.pallas_call(
        kern,
        grid=(n_batch, N_C // nh, nq),
        in_specs=[
            pl.BlockSpec(memory_space=pltpu.SMEM),
            pl.BlockSpec((blk, nh * HEAD), lambda b, h, i: (b * nq + i, COL_CQ // nh + h)),
            pl.BlockSpec((t, nh * HEAD), lambda b, h, i: (b, COL_CK // nh + h)),
            pl.BlockSpec((t, nh * HEAD), lambda b, h, i: (b, COL_CV // nh + h)),
            pl.BlockSpec((1, nh * HEAD), lambda b, h, i: (0, h)),
            pl.BlockSpec((blk, blk), lambda b, h, i: (0, 0)),
        ],
        out_specs=pl.BlockSpec((blk, nh * HEAD), lambda b, h, i: (b * nq + i, h)),
        out_shape=jax.ShapeDtypeStruct((n_batch * t, N_C * HEAD), BF16),
        compiler_params=_params(("parallel", "parallel", "arbitrary")),
        name="sb_prompt",
    )(bias, proj, proj, proj, onorm, tri)


def _sb_sample_kernel(pt_ref, *refs, n_new, scale):
    del pt_ref
    npg = PAGES_PER_STEP
    q_ref, kn_ref, vn_ref, bias_ref, on_ref, tri_ref = refs[:6]
    k_refs = refs[6:6 + npg]
    v_refs = refs[6 + npg:6 + 2 * npg]
    o_ref, acc_ref, run_ref = refs[6 + 2 * npg:]
    j = pl.program_id(1)
    rows = N_C * n_new
    n_pairs = N_C // 2
    bias = bias_ref[...]
    zq = jnp.zeros((n_new, HEAD), F32)
    q2 = []
    for g in range(n_pairs):
        qa = q_ref[:, (2 * g) * HEAD:(2 * g + 1) * HEAD]
        qb = q_ref[:, (2 * g + 1) * HEAD:(2 * g + 2) * HEAD]
        q2.append(jnp.concatenate([jnp.concatenate([qa, zq], axis=1),
                                   jnp.concatenate([zq, qb], axis=1)], axis=0).astype(BF16))

    def attend(k_blocks, v_blocks, vis):
        nb = len(k_blocks)
        z = [jnp.concatenate([_dot_nt(q2[g], k_blocks[b][g]) for g in range(n_pairs)], axis=0) * scale + bias
             for b in range(nb)]
        parts = []
        for b in range(nb):
            sp = _softplus(z[b])
            if vis is not None:
                sp = jnp.where(vis, sp, 0.0)
            parts += list(_split(sp))
        res = _dot(jnp.concatenate(parts, axis=0), tri_ref[...])
        run = run_ref[...]
        w = []
        for b in range(nb):
            r = res[2 * b * rows:(2 * b + 1) * rows] + res[(2 * b + 1) * rows:(2 * b + 2) * rows]
            wb = jnp.exp(z[b] - (r[:, :PAGE] + run))
            if vis is not None:
                wb = jnp.where(vis, wb, 0.0)
            w.append(wb)
            run = run + r[:, PAGE:]
        run_ref[...] = run
        group = 2 if nb % 2 == 0 else 1
        for b0 in range(0, nb, group):
            for g in range(n_pairs):
                sl = slice(2 * g * n_new, (2 * g + 2) * n_new)
                wg = jnp.concatenate([w[b][sl] for b in range(b0, b0 + group)], axis=1).astype(BF16)
                vg = v_blocks[b0][g] if group == 1 else jnp.concatenate(
                    [v_blocks[b][g] for b in range(b0, b0 + group)], axis=0)
                r = _dot(wg, vg)
                acc_ref[2 * g * n_new:(2 * g + 1) * n_new, :] += r[:n_new, :HEAD]
                acc_ref[(2 * g + 1) * n_new:(2 * g + 2) * n_new, :] += r[n_new:, HEAD:]

    @pl.when(j == 0)
    def _():
        acc_ref[...] = jnp.zeros_like(acc_ref)
        run_ref[...] = jnp.zeros_like(run_ref)
        pad = jnp.zeros((PAGE - n_new, 2 * HEAD), F32)
        pair = lambda ref, g: jnp.concatenate([ref[:, 2 * g * HEAD:(2 * g + 2) * HEAD], pad], axis=0).astype(BF16)
        s_idx = lax.broadcasted_iota(jnp.int32, (rows, PAGE), 1)
        i_idx = lax.broadcasted_iota(jnp.int32, (rows, PAGE), 0) % n_new
        attend([[pair(kn_ref, g) for g in range(n_pairs)]], [[pair(vn_ref, g) for g in range(n_pairs)]], s_idx < i_idx)

    def head_pair(ref, g):
        return jnp.concatenate([ref[pl.ds(2 * g, PAGE, stride=N_C), :], ref[pl.ds(2 * g + 1, PAGE, stride=N_C), :]],
                               axis=1).astype(BF16)

    attend([[head_pair(k_refs[p], g) for g in range(n_pairs)] for p in range(npg)],
           [[head_pair(v_refs[p], g) for g in range(n_pairs)] for p in range(npg)], None)

    @pl.when(j == pl.num_programs(1) - 1)
    def _():
        for h in range(N_C):
            sl = slice(h * HEAD, (h + 1) * HEAD)
            o_ref[:, sl] = _rms_rows(acc_ref[h * n_new:(h + 1) * n_new, :]) * on_ref[:, sl]


def _sb_sample(proj, cache_k, cache_v, layer, page_idx, bias_rows, onorm, n_batch, n_new, n_pages):
    npg = PAGES_PER_STEP
    steps = n_pages // npg
    width = N_C * HEAD
    rows = N_C * n_new
    idx = np.arange(PAGE)
    tri = jnp.asarray(np.concatenate([idx[:, None] >= idx[None, :], np.ones((PAGE, PAGE), bool)], axis=1), BF16)

    def page_spec(p):
        return pl.BlockSpec((None, None, PAGE * N_C, HEAD),
                            lambda b, j, pt: (layer, pt[b * n_pages + n_pages - 1 - (j * npg + p)], 0, 0))

    def const(shape):
        return pl.BlockSpec(shape, lambda b, j, pt: (0, 0))

    grid_spec = pltpu.PrefetchScalarGridSpec(
        num_scalar_prefetch=1,
        grid=(n_batch, steps),
        in_specs=[
            pl.BlockSpec((n_new, width), lambda b, j, pt: (b, COL_CQ // N_C)),
            pl.BlockSpec((n_new, width), lambda b, j, pt: (b, COL_CK // N_C)),
            pl.BlockSpec((n_new, width), lambda b, j, pt: (b, COL_CV // N_C)),
            const((rows, PAGE)), const((1, width)), const((PAGE, 2 * PAGE)),
        ] + [page_spec(p) for p in range(npg)] * 2,
        out_specs=pl.BlockSpec((n_new, width), lambda b, j, pt: (b, 0)),
        scratch_shapes=[pltpu.VMEM((rows, HEAD), F32), pltpu.VMEM((rows, PAGE), F32)],
    )
    kern = functools.partial(_sb_sample_kernel, n_new=n_new, scale=HEAD ** -0.5)
    return pl.pallas_call(
        kern,
        grid_spec=grid_spec,
        out_shape=jax.ShapeDtypeStruct((n_batch * n_new, width), F32),
        compiler_params=_params(("parallel", "arbitrary")),
        name="sb_sample",
    )(page_idx, proj, proj, proj, bias_rows, onorm, tri, *([cache_k] * npg), *([cache_v] * npg))


def _prep_w_in(w):
    d = w.shape[0]
    a, b_q, b_k, b_v, b_g, b_lr, c = (w[:, :2048], w[:, 2048:2304], w[:, 2304:2560], w[:, 2560:3072],
                                      w[:, 3072:3584], w[:, 3584:3600], w[:, 3600:])
    zq = jnp.zeros((d, HEAD - GLA_DK), w.dtype)
    bq_parts, bk_parts = [], []
    for h in range(N_B):
        sl = slice(h * GLA_DK, (h + 1) * GLA_DK)
        pad = jnp.concatenate([b_lr, zq[:, b_lr.shape[1]:]], axis=1) if h == 0 else zq
        bq_parts += [b_q[:, sl], pad]
        bk_parts += [b_k[:, sl], zq]
    out = jnp.concatenate([c, a] + bq_parts + bk_parts + [b_v, b_g], axis=1)
    assert out.shape[1] == PROJ_W
    return out.astype(BF16)


def _prep_gla_gate(w_up, b_up):
    rank = w_up.shape[0]
    wp = jnp.zeros((HEAD, N_B * HEAD), F32)
    bp = jnp.zeros((1, N_B * HEAD), F32)
    for h in range(N_B):
        wp = wp.at[LR_LANE0:LR_LANE0 + rank, h * HEAD:h * HEAD + GLA_DK].set(w_up[:, h * GLA_DK:(h + 1) * GLA_DK])
        bp = bp.at[0, h * HEAD:h * HEAD + GLA_DK].set(b_up[h * GLA_DK:(h + 1) * GLA_DK])
    return wp.astype(BF16), bp


def _pad_ff(w, axis):
    ff = w.shape[axis]
    ffp = -(-ff // FF_TILE) * FF_TILE
    pad = [(0, 0), (0, 0)]
    pad[axis] = (0, ffp - ff)
    return jnp.pad(w, pad).astype(BF16)


def kernel(x_prompt, x_sample, state_hgrn, state_gla, cache_k, cache_v, page_table, norm_ffn1, w_ffn1_gate, w_ffn1_up, w_ffn1_down, norm_mix, w_in, w_gla_alpha_up, b_gla_alpha, hgrn_lb_logits, qnorm_c, knorm_c, sb_bias, onorm_a, onorm_b, onorm_c, w_out, norm_ffn2, w_ffn2_gate, w_ffn2_up, w_ffn2_down):
    depth = w_in.shape[0]
    bp, tp, d = x_prompt.shape
    bs, ts, _ = x_sample.shape
    n_pages = page_table.shape[1]
    width = N_C * HEAD

    lb_cum = jnp.cumsum(jax.nn.softmax(hgrn_lb_logits.astype(F32), axis=0), axis=0)
    lb_all = lb_cum - lb_cum[0:1]

    xp = x_prompt.reshape(bp * tp, d)
    xs = x_sample.reshape(bs * ts, d)
    pt_flat = page_table.reshape(-1).astype(jnp.int32)
    ck = cache_k.reshape(cache_k.shape[:2] + (PAGE * N_C, HEAD))
    cv = cache_v.reshape(cache_v.shape[:2] + (PAGE * N_C, HEAD))

    consts_p = _rec_consts(REC_CHUNK, REC_CHUNK)
    consts_s = _rec_consts(SAMPLE_GROUP * ts, ts)

    outs = {k: [] for k in ("hp", "gp", "kp", "vp", "hs", "gs", "ks", "vs")}
    for l in range(depth):
        row = lambda a: a[l].reshape(1, -1).astype(F32)
        ffn1 = (row(norm_ffn1), _pad_ff(w_ffn1_gate[l], 1), _pad_ff(w_ffn1_up[l], 1), _pad_ff(w_ffn1_down[l], 0))
        ffn2 = (row(norm_ffn2), _pad_ff(w_ffn2_gate[l], 1), _pad_ff(w_ffn2_up[l], 1), _pad_ff(w_ffn2_down[l], 0))
        w_in_l = _prep_w_in(w_in[l])
        w_out_l = w_out[l].astype(BF16)
        lb = lb_all[l]
        lb3 = jnp.stack([jnp.log(lb), jnp.log1p(-lb), 1.0 - lb])
        hgrn_consts = (lb3, row(onorm_a))
        gla_consts = _prep_gla_gate(w_gla_alpha_up[l].astype(F32), b_gla_alpha[l].astype(F32)) + (row(onorm_b),)
        bias = sb_bias[l].astype(F32)
        on_c = row(onorm_c)

        xp = _ffn(xp, *ffn1)
        pj = _proj(xp, row(norm_mix), w_in_l, row(qnorm_c), row(knorm_c))
        rec = dict(chunk=REC_CHUNK, seq=REC_CHUNK, n_chunks=REC_ROWS // REC_CHUNK, seq_steps=tp // REC_ROWS, out_dtype=BF16)
        oa, hp = _recurrence("hgrn", pj, bp, None, consts_p, hgrn_consts, **rec)
        ob, gp = _recurrence("gla", pj, bp, None, consts_p, gla_consts, **rec)
        oc = _sb_prompt(pj, bias, on_c, bp, tp)
        xp = _outproj(xp, oa, ob, oc, w_out_l)
        xp = _ffn(xp, *ffn2)
        outs["hp"].append(hp)
        outs["gp"].append(gp)
        outs["kp"].append(pj[:, COL_CK * LANES:COL_CK * LANES + width].reshape(bp, tp, N_C, HEAD))
        outs["vp"].append(pj[:, COL_CV * LANES:COL_CV * LANES + width].reshape(bp, tp, N_C, HEAD))

        xs = _ffn(xs, *ffn1)
        pj = _proj(xs, row(norm_mix), w_in_l, row(qnorm_c), row(knorm_c))
        rec = dict(chunk=SAMPLE_GROUP * ts, seq=ts, n_chunks=1, seq_steps=1, out_dtype=BF16)
        oa, hs = _recurrence("hgrn", pj, bs // SAMPLE_GROUP, state_hgrn[l], consts_s, hgrn_consts, **rec)
        ob, gs = _recurrence("gla", pj, bs // SAMPLE_GROUP, state_gla[l], consts_s, gla_consts, **rec)
        bias_rows = jnp.broadcast_to(jnp.repeat(bias, ts)[:, None], (N_C * ts, PAGE))
        oc = _sb_sample(pj, ck, cv, l, pt_flat, bias_rows, on_c, bs, ts, n_pages)
        xs = _outproj(xs, oa, ob, oc, w_out_l)
        xs = _ffn(xs, *ffn2)
        outs["hs"].append(hs)
        outs["gs"].append(gs)
        outs["ks"].append(pj[:, COL_CK * LANES:COL_CK * LANES + width].reshape(bs, ts, N_C, HEAD))
        outs["vs"].append(pj[:, COL_CV * LANES:COL_CV * LANES + width].reshape(bs, ts, N_C, HEAD))

    st = lambda k: jnp.stack(outs[k])
    return (xp.reshape(bp, tp, d), xs.reshape(bs, ts, d), st("hp"), st("gp"), st("kp"), st("vp"),
            st("hs"), st("gs"), st("ks"), st("vs"))
```

```python
import functools
import math

import jax
import jax.numpy as jnp
import numpy as np
from jax import lax
from jax.experimental import pallas as pl
from jax.experimental.pallas import tpu as pltpu

F32 = jnp.float32
BF16 = jnp.bfloat16
EPS = 1e-6

LANES = 128
HEAD = 128
GLA_DK = 64
GLA_TAU = 16.0
N_A = 4
N_B = 4
N_C = 8
PAGE = 128
VMEM_LIMIT = 56 * 1024 * 1024

COL_CQ, COL_CK, COL_CV = 0, 8, 16
COL_AQ, COL_AF, COL_AI, COL_AG = 24, 28, 32, 36
COL_BQ, COL_BK, COL_BV, COL_BG = 40, 44, 48, 52
PROJ_W = 56 * LANES
LR_LANE0 = GLA_DK

FF_TILE = 512
ROW_TILE = 512
PROJ_TILE = 1024
ATT_BLOCK = 256
ATT_HEADS_PER_STEP = 4
REC_CHUNK = 64
REC_ROWS = 512
SAMPLE_GROUP = 16
PAGES_PER_STEP = 8


def _params(sem):
    return pltpu.CompilerParams(dimension_semantics=sem, vmem_limit_bytes=VMEM_LIMIT)


def _dot(a, b):
    return jnp.dot(a, b, preferred_element_type=F32)


def _dot_nt(a, b):
    return lax.dot_general(a, b, (((1,), (1,)), ((), ())), preferred_element_type=F32)


def _dot_tn(a, b):
    return lax.dot_general(a, b, (((0,), (0,)), ((), ())), preferred_element_type=F32)


def _split(x):
    hi = x.astype(BF16)
    lo = (x - hi.astype(F32)).astype(BF16)
    return hi, lo


def _log_sigmoid(x):
    return jnp.minimum(x, 0.0) - jnp.log1p(jnp.exp(-jnp.abs(x)))


def _softplus(x):
    return jnp.maximum(x, 0.0) + jnp.log1p(jnp.exp(-jnp.abs(x)))


def _rms_rows(x):
    return x * lax.rsqrt(jnp.mean(x * x, axis=-1, keepdims=True) + EPS)


def _ffn_kernel(x_ref, g_ref, wg_ref, wu_ref, wd_ref, o_ref, h_ref, acc_ref):
    f = pl.program_id(1)

    @pl.when(f == 0)
    def _():
        h_ref[...] = (_rms_rows(x_ref[...]) * g_ref[...]).astype(BF16)
        acc_ref[...] = jnp.zeros_like(acc_ref)

    h = h_ref[...]
    gate = _dot(h, wg_ref[...])
    up = _dot(h, wu_ref[...])
    act = (gate * jax.nn.sigmoid(gate) * up).astype(BF16)
    acc_ref[...] += _dot(act, wd_ref[...])

    @pl.when(f == pl.num_programs(1) - 1)
    def _():
        o_ref[...] = x_ref[...] + 0.5 * acc_ref[...]


def _ffn(x, g, wg, wu, wd):
    t, d = x.shape
    ffp = wg.shape[1]
    return pl.pallas_call(
        _ffn_kernel,
        grid=(t // ROW_TILE, ffp // FF_TILE),
        in_specs=[
            pl.BlockSpec((ROW_TILE, d), lambda i, f: (i, 0)),
            pl.BlockSpec((1, d), lambda i, f: (0, 0)),
            pl.BlockSpec((d, FF_TILE), lambda i, f: (0, f)),
            pl.BlockSpec((d, FF_TILE), lambda i, f: (0, f)),
            pl.BlockSpec((FF_TILE, d), lambda i, f: (f, 0)),
        ],
        out_specs=pl.BlockSpec((ROW_TILE, d), lambda i, f: (i, 0)),
        out_shape=jax.ShapeDtypeStruct((t, d), F32),
        scratch_shapes=[pltpu.VMEM((ROW_TILE, d), BF16), pltpu.VMEM((ROW_TILE, d), F32)],
        compiler_params=_params(("parallel", "arbitrary")),
        name="ffn",
    )(x, g, wg, wu, wd)


def _proj_kernel(x_ref, g_ref, w_ref, qn_ref, kn_ref, o_ref, h_ref):
    j = pl.program_id(1)

    @pl.when(j == 0)
    def _():
        h_ref[...] = (_rms_rows(x_ref[...]) * g_ref[...]).astype(BF16)

    acc = _dot(h_ref[...], w_ref[...])

    @pl.when(j >= 2)
    def _():
        o_ref[...] = acc

    @pl.when(j < 2)
    def _():
        gain = jnp.where(j == 0, qn_ref[...], kn_ref[...])
        for h in range(PROJ_TILE // HEAD):
            sl = slice(h * HEAD, (h + 1) * HEAD)
            o_ref[:, sl] = _rms_rows(acc[:, sl]) * gain


def _proj(x, g, w, qn, kn):
    t, d = x.shape
    return pl.pallas_call(
        _proj_kernel,
        grid=(t // ROW_TILE, PROJ_W // PROJ_TILE),
        in_specs=[
            pl.BlockSpec((ROW_TILE, d), lambda i, j: (i, 0)),
            pl.BlockSpec((1, d), lambda i, j: (0, 0)),
            pl.BlockSpec((d, PROJ_TILE), lambda i, j: (0, j)),
            pl.BlockSpec((1, HEAD), lambda i, j: (0, 0)),
            pl.BlockSpec((1, HEAD), lambda i, j: (0, 0)),
        ],
        out_specs=pl.BlockSpec((ROW_TILE, PROJ_TILE), lambda i, j: (i, j)),
        out_shape=jax.ShapeDtypeStruct((t, PROJ_W), F32),
        scratch_shapes=[pltpu.VMEM((ROW_TILE, d), BF16)],
        compiler_params=_params(("parallel", "arbitrary")),
        name="proj",
    )(x, g, w, qn, kn)


def _outproj_kernel(x_ref, a_ref, b_ref, c_ref, w_ref, o_ref):
    wa = a_ref.shape[1]
    wb = b_ref.shape[1]
    acc = _dot(a_ref[...].astype(BF16), w_ref[0:wa, :])
    acc += _dot(b_ref[...].astype(BF16), w_ref[wa:wa + wb, :])
    acc += _dot(c_ref[...].astype(BF16), w_ref[wa + wb:, :])
    o_ref[...] = x_ref[...] + acc


def _outproj(x, oa, ob, oc, w):
    t, d = x.shape
    return pl.pallas_call(
        _outproj_kernel,
        grid=(t // ROW_TILE,),
        in_specs=[
            pl.BlockSpec((ROW_TILE, d), lambda i: (i, 0)),
            pl.BlockSpec((ROW_TILE, oa.shape[1]), lambda i: (i, 0)),
            pl.BlockSpec((ROW_TILE, ob.shape[1]), lambda i: (i, 0)),
            pl.BlockSpec((ROW_TILE, oc.shape[1]), lambda i: (i, 0)),
            pl.BlockSpec(w.shape, lambda i: (0, 0)),
        ],
        out_specs=pl.BlockSpec((ROW_TILE, d), lambda i: (i, 0)),
        out_shape=jax.ShapeDtypeStruct((t, d), F32),
        compiler_params=_params(("parallel",)),
        name="outproj",
    )(x, oa, ob, oc, w)


def _rec_consts(chunk, seq):
    levels = int(math.log2(seq))
    t = np.arange(chunk)
    col = t[None, :]
    row = t[:, None]
    w = np.zeros((levels + 2, chunk, chunk), np.float32)
    for l in range(levels):
        m = 1 << l
        same = (row // m) == (col // m)
        right = ((row // m) % 2) == 1
        w[l] = np.where(right, same & (col <= row), same & (col > row))
    same_seq = (row // seq) == (col // seq)
    w[levels] = same_seq & (col <= row)
    w[levels + 1] = same_seq & (col > row)
    lv = np.full((chunk, chunk), -1, np.int32)
    strict = same_seq & (col < row)
    x = np.bitwise_xor(row, col)
    lv[strict] = np.floor(np.log2(np.maximum(x, 1)))[strict].astype(np.int32)
    lv[row == col] = levels
    return jnp.asarray(w.reshape(-1, chunk), BF16), jnp.asarray(lv), levels


def _rec_kernel(*refs, kind, chunk, seq, n_chunks, levels, has_s0, dk, n_heads, carry):
    refs = list(refs)
    if kind == "hgrn":
        q_ref, f_ref, v_ref, g_ref, lb_ref, on_ref, w_ref, lv_ref = refs[:8]
        rest = refs[8:]
    else:
        q_ref, k_ref, v_ref, g_ref, wup_ref, bup_ref, on_ref, w_ref, lv_ref = refs[:9]
        rest = refs[9:]
    s0_ref = rest.pop(0) if has_s0 else None
    o_ref, s_ref = rest[:2]
    st_ref = rest[2] if carry else None
    n_seq = chunk // seq
    step = pl.program_id(1)
    ones_bf = jnp.ones((seq, HEAD), BF16)

    def gates(rows, hh):
        cols = slice(hh * HEAD, (hh + 1) * HEAD)
        if kind == "hgrn":
            fr = f_ref[rows, cols]
            ls = _log_sigmoid(fr)
            x1 = lb_ref[0:1, cols]
            x2 = lb_ref[1:2, cols] + ls
            la = jnp.maximum(x1, x2) + jnp.log1p(jnp.exp(-jnp.abs(x1 - x2)))
            k = lb_ref[2:3, cols] * jnp.exp(ls - fr)
            q = q_ref[rows, cols]
            gate = jax.nn.sigmoid(g_ref[rows, cols])
        else:
            x = _dot(q_ref[rows, 0:HEAD].astype(BF16), wup_ref[:, cols]) + bup_ref[:, cols]
            la = _log_sigmoid(x) * (1.0 / GLA_TAU)
            k = k_ref[rows, cols]
            q = q_ref[rows, cols] * (GLA_DK ** -0.5)
            g = g_ref[rows, cols]
            gate = g * jax.nn.sigmoid(g)
        return q, k, v_ref[rows, cols].astype(BF16), la, gate

    def do_chunk(r0, states):
        rows = pl.ds(r0, chunk)
        hs = range(n_heads)
        w = w_ref[...]
        lv = lv_ref[...]
        q, k, vb, la, gate = zip(*[gates(rows, hh) for hh in hs])
        hl = [_split(la[hh]) for hh in hs]
        ex = [jnp.exp(_dot(w, hl[hh][0]) + _dot(w, hl[hh][1])) for hh in hs]
        p = [[_dot_nt(q[hh].astype(BF16), k[hh].astype(BF16))] for hh in hs]
        for l in range(levels):
            for hh in hs:
                xl = ex[hh][l * chunk:(l + 1) * chunk]
                p[hh].append(_dot_nt((q[hh] * xl).astype(BF16), (k[hh] * xl).astype(BF16)))
        att = []
        for hh in hs:
            a = jnp.where(lv == levels, p[hh][0], 0.0)
            for l in range(levels):
                a = jnp.where(lv == l, p[hh][l + 1], a)
            att.append(a.astype(BF16))
        qt = [q[hh] * ex[hh][levels * chunk:(levels + 1) * chunk] for hh in hs]
        kt = [k[hh] * ex[hh][(levels + 1) * chunk:] for hh in hs]
        o = [_dot(att[hh], vb[hh]) for hh in hs]
        inter = [[_dot(qt[hh][n * seq:(n + 1) * seq].astype(BF16), states[hh * n_seq + n].astype(BF16))
                  for n in range(n_seq)] for hh in hs]
        new_states = []
        for hh in hs:
            for n in range(n_seq):
                sl = slice(n * seq, (n + 1) * seq)
                h_n, l_n = hl[hh][0][sl], hl[hh][1][sl]
                decay = jnp.exp(_dot_tn(h_n, ones_bf) + _dot_tn(l_n, ones_bf))
                new_states.append(decay * states[hh * n_seq + n] + _dot_tn(kt[hh][sl].astype(BF16), vb[hh][sl]))
        for hh in hs:
            cols = slice(hh * HEAD, (hh + 1) * HEAD)
            oh = o[hh] + (inter[hh][0] if n_seq == 1 else jnp.concatenate(inter[hh], axis=0))
            o_ref[rows, cols] = (_rms_rows(oh) * on_ref[:, cols] * gate[hh]).astype(o_ref.dtype)
        return tuple(new_states)

    def initial(hh, n):
        if s0_ref is None:
            return jnp.zeros((HEAD, HEAD), F32)
        s = s0_ref[n, hh]
        if dk < HEAD:
            s = jnp.concatenate([s, jnp.zeros((HEAD - dk, HEAD), F32)], axis=0)
        return s

    if carry:
        @pl.when(step == 0)
        def _():
            for hh in range(n_heads):
                st_ref[hh] = initial(hh, 0)
        states = tuple(st_ref[hh] for hh in range(n_heads))
    else:
        states = tuple(initial(hh, n) for hh in range(n_heads) for n in range(n_seq))
    if n_chunks == 1:
        states = do_chunk(0, states)
    else:
        states = lax.fori_loop(
            0, n_chunks, lambda c, st: do_chunk(pl.multiple_of(c * chunk, chunk), st), states)
    if carry:
        for hh in range(n_heads):
            st_ref[hh] = states[hh]

        @pl.when(step == pl.num_programs(1) - 1)
        def _():
            for hh in range(n_heads):
                s_ref[0, hh] = states[hh][:dk]
    else:
        for hh in range(n_heads):
            for n in range(n_seq):
                s_ref[n, hh] = states[hh * n_seq + n][:dk]


def _recurrence(kind, proj, n_groups, s0, consts, layer_consts, *, chunk, seq, n_chunks, seq_steps, out_dtype):
    w, lv, levels = consts
    rows = chunk * n_chunks
    n_seq = chunk // seq
    dk = HEAD if kind == "hgrn" else GLA_DK
    n_heads = N_A if kind == "hgrn" else N_B
    carry = seq_steps > 1
    assert n_seq == 1 or not carry
    assert s0 is None or not carry

    def col(c0):
        return pl.BlockSpec((rows, n_heads * HEAD), lambda g, t: (g * seq_steps + t, c0 // n_heads))

    def whole(a):
        return pl.BlockSpec(a.shape, lambda g, t: (0,) * a.ndim)

    if kind == "hgrn":
        args = [proj, proj, proj, proj, *layer_consts, w, lv]
        specs = [col(COL_AQ), col(COL_AF), col(COL_AI), col(COL_AG)]
    else:
        args = [proj, proj, proj, proj, *layer_consts, w, lv]
        specs = [col(COL_BQ), col(COL_BK), col(COL_BV), col(COL_BG)]
    specs += [whole(a) for a in args[4:]]
    state_spec = pl.BlockSpec((n_seq, n_heads, dk, HEAD), lambda g, t: (g, 0, 0, 0))
    if s0 is not None:
        args.append(s0)
        specs.append(state_spec)
    kern = functools.partial(_rec_kernel, kind=kind, chunk=chunk, seq=seq, n_chunks=n_chunks, levels=levels,
                             has_s0=s0 is not None, dk=dk, n_heads=n_heads, carry=carry)
    return pl.pallas_call(
        kern,
        grid=(n_groups, seq_steps),
        in_specs=specs,
        out_specs=[pl.BlockSpec((rows, n_heads * HEAD), lambda g, t: (g * seq_steps + t, 0)), state_spec],
        out_shape=[jax.ShapeDtypeStruct((n_groups * seq_steps * rows, n_heads * HEAD), out_dtype),
                   jax.ShapeDtypeStruct((n_groups * n_seq, n_heads, dk, HEAD), F32)],
        scratch_shapes=[pltpu.VMEM((n_heads, HEAD, HEAD), F32)] if carry else [],
        compiler_params=_params(("parallel", "arbitrary")),
        name="rec_" + kind,
    )(*args)


def _tri_suffix(n):
    j = np.arange(n)
    return jnp.asarray(j[:, None] >= j[None, :], BF16)


def _sb_prompt_kernel(bias_ref, q_ref, k_ref, v_ref, on_ref, tri_ref, o_ref, *, blk, scale, n_heads):
    h0 = pl.program_id(1) * n_heads
    qb = pl.program_id(2)
    tri = tri_ref[...]
    cols = [slice(i * HEAD, (i + 1) * HEAD) for i in range(n_heads)]
    q = [q_ref[:, c].astype(BF16) for c in cols]
    bias = [bias_ref[h0 + i] for i in range(n_heads)]

    def block(kb, masked, carry):
        rows = pl.ds(pl.multiple_of(kb * blk, blk), blk)
        hs = range(n_heads)
        z = [_dot_nt(q[i], k_ref[rows, cols[i]].astype(BF16)) * scale + bias[i] for i in hs]
        sp = [_softplus(z[i]) for i in hs]
        if masked:
            vis = lax.broadcasted_iota(jnp.int32, (blk, blk), 1) < lax.broadcasted_iota(jnp.int32, (blk, blk), 0)
            sp = [jnp.where(vis, s, 0.0) for s in sp]
        cs = [_dot(jnp.concatenate(_split(sp[i]), axis=1), tri) for i in hs]
        w = [jnp.exp(z[i] - (cs[i] + carry[i][1])) for i in hs]
        if masked:
            w = [jnp.where(vis, x, 0.0) for x in w]
        acc = [carry[i][0] + _dot(w[i].astype(BF16), v_ref[rows, cols[i]].astype(BF16)) for i in hs]
        return tuple((acc[i], carry[i][1] + cs[i][:, 0:1]) for i in hs)

    zero = (jnp.zeros((blk, HEAD), F32), jnp.zeros((blk, 1), F32))
    carry = block(qb, True, (zero,) * n_heads)
    carry = lax.fori_loop(1, qb + 1, lambda j, c: block(qb - j, False, c), carry)
    for i in range(n_heads):
        o_ref[:, cols[i]] = (_rms_rows(carry[i][0]) * on_ref[:, cols[i]]).astype(o_ref.dtype)


def _sb_prompt(proj, bias, onorm, n_batch, t):
    blk = ATT_BLOCK
    nq = t // blk
    nh = ATT_HEADS_PER_STEP
    tri = _tri_suffix(blk)
    tri = jnp.concatenate([tri, tri], axis=0)
    kern = functools.partial(_sb_prompt_kernel, blk=blk, scale=HEAD ** -0.5, n_heads=nh)
    return pl.pallas_call(
        kern,
        grid=(n_batch, N_C // nh, nq),
        in_specs=[
            pl.BlockSpec(memory_space=pltpu.SMEM),
            pl.BlockSpec((blk, nh * HEAD), lambda b, h, i: (b * nq + i, COL_CQ // nh + h)),
            pl.BlockSpec((t, nh * HEAD), lambda b, h, i: (b, COL_CK // nh + h)),
            pl.BlockSpec((t, nh * HEAD), lambda b, h, i: (b, COL_CV // nh + h)),
            pl.BlockSpec((1, nh * HEAD), lambda b, h, i: (0, h)),
            pl.BlockSpec((2 * blk, blk), lambda b, h, i: (0, 0)),
        ],
        out_specs=pl.BlockSpec((blk, nh * HEAD), lambda b, h, i: (b * nq + i, h)),
        out_shape=jax.ShapeDtypeStruct((n_batch * t, N_C * HEAD), BF16),
        compiler_params=_params(("parallel", "parallel", "arbitrary")),
        name="sb_prompt",
    )(bias, proj, proj, proj, onorm, tri)


def _sb_sample_kernel(pt_ref, *refs, n_new, scale):
    del pt_ref
    npg = PAGES_PER_STEP
    q_ref, kn_ref, vn_ref, bias_ref, on_ref, tri_ref = refs[:6]
    k_refs = refs[6:6 + npg]
    v_refs = refs[6 + npg:6 + 2 * npg]
    o_ref, acc_ref, run_ref = refs[6 + 2 * npg:]
    j = pl.program_id(1)
    rows = N_C * n_new
    n_pairs = N_C // 2
    bias = bias_ref[...]
    zq = jnp.zeros((n_new, HEAD), F32)
    q2 = []
    for g in range(n_pairs):
        qa = q_ref[:, (2 * g) * HEAD:(2 * g + 1) * HEAD]
        qb = q_ref[:, (2 * g + 1) * HEAD:(2 * g + 2) * HEAD]
        q2.append(jnp.concatenate([jnp.concatenate([qa, zq], axis=1),
                                   jnp.concatenate([zq, qb], axis=1)], axis=0).astype(BF16))

    def attend(k_blocks, v_blocks, vis):
        nb = len(k_blocks)
        z = [jnp.concatenate([_dot_nt(q2[g], k_blocks[b][g]) for g in range(n_pairs)], axis=0) * scale + bias
             for b in range(nb)]
        parts = []
        for b in range(nb):
            sp = _softplus(z[b])
            if vis is not None:
                sp = jnp.where(vis, sp, 0.0)
            parts += list(_split(sp))
        res = _dot(jnp.concatenate(parts, axis=0), tri_ref[...])
        run = run_ref[...]
        w = []
        for b in range(nb):
            r = res[2 * b * rows:(2 * b + 1) * rows] + res[(2 * b + 1) * rows:(2 * b + 2) * rows]
            wb = jnp.exp(z[b] - (r[:, :PAGE] + run))
            if vis is not None:
                wb = jnp.where(vis, wb, 0.0)
            w.append(wb)
            run = run + r[:, PAGE:]
        run_ref[...] = run
        group = 2 if nb % 2 == 0 else 1
        for b0 in range(0, nb, group):
            for g in range(n_pairs):
                sl = slice(2 * g * n_new, (2 * g + 2) * n_new)
                wg = jnp.concatenate([w[b][sl] for b in range(b0, b0 + group)], axis=1).astype(BF16)
                vg = v_blocks[b0][g] if group == 1 else jnp.concatenate(
                    [v_blocks[b][g] for b in range(b0, b0 + group)], axis=0)
                r = _dot(wg, vg)
                acc_ref[2 * g * n_new:(2 * g + 1) * n_new, :] += r[:n_new, :HEAD]
                acc_ref[(2 * g + 1) * n_new:(2 * g + 2) * n_new, :] += r[n_new:, HEAD:]

    @pl.when(j == 0)
    def _():
        acc_ref[...] = jnp.zeros_like(acc_ref)
        run_ref[...] = jnp.zeros_like(run_ref)
        pad = jnp.zeros((PAGE - n_new, 2 * HEAD), F32)
        pair = lambda ref, g: jnp.concatenate([ref[:, 2 * g * HEAD:(2 * g + 2) * HEAD], pad], axis=0).astype(BF16)
        s_idx = lax.broadcasted_iota(jnp.int32, (rows, PAGE), 1)
        i_idx = lax.broadcasted_iota(jnp.int32, (rows, PAGE), 0) % n_new
        attend([[pair(kn_ref, g) for g in range(n_pairs)]], [[pair(vn_ref, g) for g in range(n_pairs)]], s_idx < i_idx)

    def head_pair(ref, g):
        return jnp.concatenate([ref[pl.ds(2 * g, PAGE, stride=N_C), :], ref[pl.ds(2 * g + 1, PAGE, stride=N_C), :]],
                               axis=1).astype(BF16)

    attend([[head_pair(k_refs[p], g) for g in range(n_pairs)] for p in range(npg)],
           [[head_pair(v_refs[p], g) for g in range(n_pairs)] for p in range(npg)], None)

    @pl.when(j == pl.num_programs(1) - 1)
    def _():
        for h in range(N_C):
            sl = slice(h * HEAD, (h + 1) * HEAD)
            o_ref[:, sl] = _rms_rows(acc_ref[h * n_new:(h + 1) * n_new, :]) * on_ref[:, sl]


def _sb_sample(proj, cache_k, cache_v, layer, page_idx, bias_rows, onorm, n_batch, n_new, n_pages):
    npg = PAGES_PER_STEP
    steps = n_pages // npg
    width = N_C * HEAD
    rows = N_C * n_new
    idx = np.arange(PAGE)
    tri = jnp.asarray(np.concatenate([idx[:, None] >= idx[None, :], np.ones((PAGE, PAGE), bool)], axis=1), BF16)

    def page_spec(p):
        return pl.BlockSpec((None, None, PAGE * N_C, HEAD),
                            lambda b, j, pt: (layer, pt[b * n_pages + n_pages - 1 - (j * npg + p)], 0, 0))

    def const(shape):
        return pl.BlockSpec(shape, lambda b, j, pt: (0, 0))

    grid_spec = pltpu.PrefetchScalarGridSpec(
        num_scalar_prefetch=1,
        grid=(n_batch, steps),
        in_specs=[
            pl.BlockSpec((n_new, width), lambda b, j, pt: (b, COL_CQ // N_C)),
            pl.BlockSpec((n_new, width), lambda b, j, pt: (b, COL_CK // N_C)),
            pl.BlockSpec((n_new, width), lambda b, j, pt: (b, COL_CV // N_C)),
            const((rows, PAGE)), const((1, width)), const((PAGE, 2 * PAGE)),
        ] + [page_spec(p) for p in range(npg)] * 2,
        out_specs=pl.BlockSpec((n_new, width), lambda b, j, pt: (b, 0)),
        scratch_shapes=[pltpu.VMEM((rows, HEAD), F32), pltpu.VMEM((rows, PAGE), F32)],
    )
    kern = functools.partial(_sb_sample_kernel, n_new=n_new, scale=HEAD ** -0.5)
    return pl.pallas_call(
        kern,
        grid_spec=grid_spec,
        out_shape=jax.ShapeDtypeStruct((n_batch * n_new, width), F32),
        compiler_params=_params(("parallel", "arbitrary")),
        name="sb_sample",
    )(page_idx, proj, proj, proj, bias_rows, onorm, tri, *([cache_k] * npg), *([cache_v] * npg))


def _prep_w_in(w):
    d = w.shape[0]
    a, b_q, b_k, b_v, b_g, b_lr, c = (w[:, :2048], w[:, 2048:2304], w[:, 2304:2560], w[:, 2560:3072],
                                      w[:, 3072:3584], w[:, 3584:3600], w[:, 3600:])
    zq = jnp.zeros((d, HEAD - GLA_DK), w.dtype)
    bq_parts, bk_parts = [], []
    for h in range(N_B):
        sl = slice(h * GLA_DK, (h + 1) * GLA_DK)
        pad = jnp.concatenate([b_lr, zq[:, b_lr.shape[1]:]], axis=1) if h == 0 else zq
        bq_parts += [b_q[:, sl], pad]
        bk_parts += [b_k[:, sl], zq]
    out = jnp.concatenate([c, a] + bq_parts + bk_parts + [b_v, b_g], axis=1)
    assert out.shape[1] == PROJ_W
    return out.astype(BF16)


def _prep_gla_gate(w_up, b_up):
    rank = w_up.shape[0]
    wp = jnp.zeros((HEAD, N_B * HEAD), F32)
    bp = jnp.zeros((1, N_B * HEAD), F32)
    for h in range(N_B):
        wp = wp.at[LR_LANE0:LR_LANE0 + rank, h * HEAD:h * HEAD + GLA_DK].set(w_up[:, h * GLA_DK:(h + 1) * GLA_DK])
        bp = bp.at[0, h * HEAD:h * HEAD + GLA_DK].set(b_up[h * GLA_DK:(h + 1) * GLA_DK])
    return wp.astype(BF16), bp


def _pad_ff(w, axis):
    ff = w.shape[axis]
    ffp = -(-ff // FF_TILE) * FF_TILE
    pad = [(0, 0), (0, 0)]
    pad[axis] = (0, ffp - ff)
    return jnp.pad(w, pad).astype(BF16)


def kernel(x_prompt, x_sample, state_hgrn, state_gla, cache_k, cache_v, page_table, norm_ffn1, w_ffn1_gate, w_ffn1_up, w_ffn1_down, norm_mix, w_in, w_gla_alpha_up, b_gla_alpha, hgrn_lb_logits, qnorm_c, knorm_c, sb_bias, onorm_a, onorm_b, onorm_c, w_out, norm_ffn2, w_ffn2_gate, w_ffn2_up, w_ffn2_down):
    depth = w_in.shape[0]
    bp, tp, d = x_prompt.shape
    bs, ts, _ = x_sample.shape
    n_pages = page_table.shape[1]
    width = N_C * HEAD

    lb_cum = jnp.cumsum(jax.nn.softmax(hgrn_lb_logits.astype(F32), axis=0), axis=0)
    lb_all = lb_cum - lb_cum[0:1]

    xp = x_prompt.reshape(bp * tp, d)
    xs = x_sample.reshape(bs * ts, d)
    pt_flat = page_table.reshape(-1).astype(jnp.int32)
    ck = cache_k.reshape(cache_k.shape[:2] + (PAGE * N_C, HEAD))
    cv = cache_v.reshape(cache_v.shape[:2] + (PAGE * N_C, HEAD))

    consts_p = _rec_consts(REC_CHUNK, REC_CHUNK)
    consts_s = _rec_consts(SAMPLE_GROUP * ts, ts)

    outs = {k: [] for k in ("hp", "gp", "kp", "vp", "hs", "gs", "ks", "vs")}
    for l in range(depth):
        row = lambda a: a[l].reshape(1, -1).astype(F32)
        ffn1 = (row(norm_ffn1), _pad_ff(w_ffn1_gate[l], 1), _pad_ff(w_ffn1_up[l], 1), _pad_ff(w_ffn1_down[l], 0))
        ffn2 = (row(norm_ffn2), _pad_ff(w_ffn2_gate[l], 1), _pad_ff(w_ffn2_up[l], 1), _pad_ff(w_ffn2_down[l], 0))
        w_in_l = _prep_w_in(w_in[l])
        w_out_l = w_out[l].astype(BF16)
        lb = lb_all[l]
        lb3 = jnp.stack([jnp.log(lb), jnp.log1p(-lb), 1.0 - lb])
        hgrn_consts = (lb3, row(onorm_a))
        gla_consts = _prep_gla_gate(w_gla_alpha_up[l].astype(F32), b_gla_alpha[l].astype(F32)) + (row(onorm_b),)
        bias = sb_bias[l].astype(F32)
        on_c = row(onorm_c)

        xp = _ffn(xp, *ffn1)
        pj = _proj(xp, row(norm_mix), w_in_l, row(qnorm_c), row(knorm_c))
        rec = dict(chunk=REC_CHUNK, seq=REC_CHUNK, n_chunks=REC_ROWS // REC_CHUNK, seq_steps=tp // REC_ROWS, out_dtype=BF16)
        oa, hp = _recurrence("hgrn", pj, bp, None, consts_p, hgrn_consts, **rec)
        ob, gp = _recurrence("gla", pj, bp, None, consts_p, gla_consts, **rec)
        oc = _sb_prompt(pj, bias, on_c, bp, tp)
        xp = _outproj(xp, oa, ob, oc, w_out_l)
        xp = _ffn(xp, *ffn2)
        outs["hp"].append(hp)
        outs["gp"].append(gp)
        outs["kp"].append(pj[:, COL_CK * LANES:COL_CK * LANES + width].reshape(bp, tp, N_C, HEAD))
        outs["vp"].append(pj[:, COL_CV * LANES:COL_CV * LANES + width].reshape(bp, tp, N_C, HEAD))

        xs = _ffn(xs, *ffn1)
        pj = _proj(xs, row(norm_mix), w_in_l, row(qnorm_c), row(knorm_c))
        rec = dict(chunk=SAMPLE_GROUP * ts, seq=ts, n_chunks=1, seq_steps=1, out_dtype=BF16)
        oa, hs = _recurrence("hgrn", pj, bs // SAMPLE_GROUP, state_hgrn[l], consts_s, hgrn_consts, **rec)
        ob, gs = _recurrence("gla", pj, bs // SAMPLE_GROUP, state_gla[l], consts_s, gla_consts, **rec)
        bias_rows = jnp.broadcast_to(jnp.repeat(bias, ts)[:, None], (N_C * ts, PAGE))
        oc = _sb_sample(pj, ck, cv, l, pt_flat, bias_rows, on_c, bs, ts, n_pages)
        xs = _outproj(xs, oa, ob, oc, w_out_l)
        xs = _ffn(xs, *ffn2)
        outs["hs"].append(hs)
        outs["gs"].append(gs)
        outs["ks"].append(pj[:, COL_CK * LANES:COL_CK * LANES + width].reshape(bs, ts, N_C, HEAD))
        outs["vs"].append(pj[:, COL_CV * LANES:COL_CV * LANES + width].reshape(bs, ts, N_C, HEAD))

    st = lambda k: jnp.stack(outs[k])
    return (xp.reshape(bp, tp, d), xs.reshape(bs, ts, d), st("hp"), st("gp"), st("kp"), st("vp"),
            st("hs"), st("gs"), st("ks"), st("vs"))
```

```python
import functools
import math

import jax
import jax.numpy as jnp
import numpy as np
from jax import lax
from jax.experimental import pallas as pl
from jax.experimental.pallas import tpu as pltpu

F32 = jnp.float32
BF16 = jnp.bfloat16
EPS = 1e-6

LANES = 128
HEAD = 128
GLA_DK = 64
GLA_TAU = 16.0
N_A = 4
N_B = 4
N_C = 8
PAGE = 128
VMEM_LIMIT = 56 * 1024 * 1024

COL_CQ, COL_CK, COL_CV = 0, 8, 16
COL_AQ, COL_AF, COL_AI, COL_AG = 24, 28, 32, 36
COL_BQ, COL_BK, COL_BV, COL_BG = 40, 44, 48, 52
PROJ_W = 56 * LANES
LR_LANE0 = GLA_DK

FF_TILE = 512
ROW_TILE = 512
PROJ_TILE = 1024
ATT_BLOCK = 256
ATT_HEADS_PER_STEP = 4
REC_CHUNK = 64
REC_ROWS = 512
SAMPLE_GROUP = 16
PAGES_PER_STEP = 8


def _params(sem):
    return pltpu.CompilerParams(dimension_semantics=sem, vmem_limit_bytes=VMEM_LIMIT)


def _dot(a, b):
    return jnp.dot(a, b, preferred_element_type=F32)


def _dot_nt(a, b):
    return lax.dot_general(a, b, (((1,), (1,)), ((), ())), preferred_element_type=F32)


def _dot_tn(a, b):
    return lax.dot_general(a, b, (((0,), (0,)), ((), ())), preferred_element_type=F32)


def _split(x):
    hi = x.astype(BF16)
    lo = (x - hi.astype(F32)).astype(BF16)
    return hi, lo


def _log_sigmoid(x):
    return jnp.minimum(x, 0.0) - jnp.log1p(jnp.exp(-jnp.abs(x)))


def _softplus(x):
    return jnp.maximum(x, 0.0) + jnp.log1p(jnp.exp(-jnp.abs(x)))


def _rms_rows(x):
    return x * lax.rsqrt(jnp.mean(x * x, axis=-1, keepdims=True) + EPS)


def _ffn_kernel(x_ref, g_ref, wg_ref, wu_ref, wd_ref, o_ref, h_ref, acc_ref):
    f = pl.program_id(1)

    @pl.when(f == 0)
    def _():
        h_ref[...] = (_rms_rows(x_ref[...]) * g_ref[...]).astype(BF16)
        acc_ref[...] = jnp.zeros_like(acc_ref)

    h = h_ref[...]
    gate = _dot(h, wg_ref[...])
    up = _dot(h, wu_ref[...])
    act = (gate * jax.nn.sigmoid(gate) * up).astype(BF16)
    acc_ref[...] += _dot(act, wd_ref[...])

    @pl.when(f == pl.num_programs(1) - 1)
    def _():
        o_ref[...] = x_ref[...] + 0.5 * acc_ref[...]


def _ffn(x, g, wg, wu, wd, layer):
    t, d = x.shape
    ffp = wg.shape[2]
    return pl.pallas_call(
        _ffn_kernel,
        grid=(t // ROW_TILE, ffp // FF_TILE),
        in_specs=[
            pl.BlockSpec((ROW_TILE, d), lambda i, f: (i, 0)),
            pl.BlockSpec((1, d), lambda i, f: (0, 0)),
            pl.BlockSpec((None, d, FF_TILE), lambda i, f: (layer, 0, f)),
            pl.BlockSpec((None, d, FF_TILE), lambda i, f: (layer, 0, f)),
            pl.BlockSpec((None, FF_TILE, d), lambda i, f: (layer, f, 0)),
        ],
        out_specs=pl.BlockSpec((ROW_TILE, d), lambda i, f: (i, 0)),
        out_shape=jax.ShapeDtypeStruct((t, d), F32),
        scratch_shapes=[pltpu.VMEM((ROW_TILE, d), BF16), pltpu.VMEM((ROW_TILE, d), F32)],
        compiler_params=_params(("parallel", "arbitrary")),
        name="ffn",
    )(x, g, wg, wu, wd)


def _proj_kernel(x_ref, g_ref, w_ref, qn_ref, kn_ref, o_ref, k_ref, v_ref, h_ref):
    j = pl.program_id(1)

    @pl.when(j == 0)
    def _():
        h_ref[...] = (_rms_rows(x_ref[...]) * g_ref[...]).astype(BF16)

    acc = _dot(h_ref[...], w_ref[...])

    @pl.when(j >= 2)
    def _():
        o_ref[...] = acc

    @pl.when(j == 2)
    def _():
        v_ref[...] = acc

    @pl.when(j < 2)
    def _():
        gain = jnp.where(j == 0, qn_ref[...], kn_ref[...])
        for h in range(PROJ_TILE // HEAD):
            sl = slice(h * HEAD, (h + 1) * HEAD)
            o_ref[:, sl] = _rms_rows(acc[:, sl]) * gain

    @pl.when(j == 1)
    def _():
        k_ref[...] = o_ref[...]


def _proj(x, g, w, qn, kn, layer):
    t, d = x.shape
    rows = ROW_TILE
    kv_spec = pl.BlockSpec((rows, PROJ_TILE), lambda i, j: (i, 0))
    kv_shape = jax.ShapeDtypeStruct((t, PROJ_TILE), F32)
    return pl.pallas_call(
        _proj_kernel,
        grid=(t // rows, PROJ_W // PROJ_TILE),
        in_specs=[
            pl.BlockSpec((rows, d), lambda i, j: (i, 0)),
            pl.BlockSpec((1, d), lambda i, j: (0, 0)),
            pl.BlockSpec((None, d, PROJ_TILE), lambda i, j: (layer, 0, j)),
            pl.BlockSpec((1, HEAD), lambda i, j: (0, 0)),
            pl.BlockSpec((1, HEAD), lambda i, j: (0, 0)),
        ],
        out_specs=[pl.BlockSpec((rows, PROJ_TILE), lambda i, j: (i, j)), kv_spec, kv_spec],
        out_shape=[jax.ShapeDtypeStruct((t, PROJ_W), F32), kv_shape, kv_shape],
        scratch_shapes=[pltpu.VMEM((rows, d), BF16)],
        compiler_params=_params(("parallel", "arbitrary")),
        name="proj",
    )(x, g, w, qn, kn)


def _outproj_kernel(x_ref, a_ref, b_ref, c_ref, w_ref, o_ref):
    wa = a_ref.shape[1]
    wb = b_ref.shape[1]
    acc = _dot(a_ref[...].astype(BF16), w_ref[0:wa, :])
    acc += _dot(b_ref[...].astype(BF16), w_ref[wa:wa + wb, :])
    acc += _dot(c_ref[...].astype(BF16), w_ref[wa + wb:, :])
    o_ref[...] = x_ref[...] + acc


def _outproj(x, oa, ob, oc, w, layer):
    t, d = x.shape
    return pl.pallas_call(
        _outproj_kernel,
        grid=(t // ROW_TILE,),
        in_specs=[
            pl.BlockSpec((ROW_TILE, d), lambda i: (i, 0)),
            pl.BlockSpec((ROW_TILE, oa.shape[1]), lambda i: (i, 0)),
            pl.BlockSpec((ROW_TILE, ob.shape[1]), lambda i: (i, 0)),
            pl.BlockSpec((ROW_TILE, oc.shape[1]), lambda i: (i, 0)),
            pl.BlockSpec((None,) + w.shape[1:], lambda i: (layer, 0, 0)),
        ],
        out_specs=pl.BlockSpec((ROW_TILE, d), lambda i: (i, 0)),
        out_shape=jax.ShapeDtypeStruct((t, d), F32),
        compiler_params=_params(("parallel",)),
        name="outproj",
    )(x, oa, ob, oc, w)


def _rec_consts(chunk, seq):
    levels = int(math.log2(seq))
    t = np.arange(chunk)
    col = t[None, :]
    row = t[:, None]
    w = np.zeros((levels + 2, chunk, chunk), np.float32)
    for l in range(levels):
        m = 1 << l
        same = (row // m) == (col // m)
        right = ((row // m) % 2) == 1
        w[l] = np.where(right, same & (col <= row), same & (col > row))
    same_seq = (row // seq) == (col // seq)
    w[levels] = same_seq & (col <= row)
    w[levels + 1] = same_seq & (col > row)
    lv = np.full((chunk, chunk), -1, np.int32)
    strict = same_seq & (col < row)
    x = np.bitwise_xor(row, col)
    lv[strict] = np.floor(np.log2(np.maximum(x, 1)))[strict].astype(np.int32)
    lv[row == col] = levels
    return jnp.asarray(w.reshape(-1, chunk), BF16), jnp.asarray(lv), levels


def _rec_kernel(*refs, kind, chunk, seq, n_chunks, levels, has_s0, dk, n_heads, carry):
    refs = list(refs)
    if kind == "hgrn":
        q_ref, f_ref, v_ref, g_ref, lb_ref, on_ref, w_ref, lv_ref = refs[:8]
        rest = refs[8:]
    else:
        q_ref, k_ref, v_ref, g_ref, wup_ref, bup_ref, on_ref, w_ref, lv_ref = refs[:9]
        rest = refs[9:]
    s0_ref = rest.pop(0) if has_s0 else None
    o_ref, s_ref = rest[:2]
    st_ref = rest[2] if carry else None
    n_seq = chunk // seq
    step = pl.program_id(1)
    ones_bf = jnp.ones((seq, HEAD), BF16)

    def gates(rows, hh):
        cols = slice(hh * HEAD, (hh + 1) * HEAD)
        if kind == "hgrn":
            fr = f_ref[rows, cols]
            ls = _log_sigmoid(fr)
            x1 = lb_ref[0:1, cols]
            x2 = lb_ref[1:2, cols] + ls
            la = jnp.maximum(x1, x2) + jnp.log1p(jnp.exp(-jnp.abs(x1 - x2)))
            k = lb_ref[2:3, cols] * jnp.exp(ls - fr)
            q = q_ref[rows, cols]
            gate = jax.nn.sigmoid(g_ref[rows, cols])
        else:
            x = _dot(q_ref[rows, 0:HEAD].astype(BF16), wup_ref[:, cols]) + bup_ref[:, cols]
            la = _log_sigmoid(x) * (1.0 / GLA_TAU)
            k = k_ref[rows, cols]
            q = q_ref[rows, cols] * (GLA_DK ** -0.5)
            g = g_ref[rows, cols]
            gate = g * jax.nn.sigmoid(g)
        return q, k, v_ref[rows, cols].astype(BF16), la, gate

    def do_chunk(r0, states):
        rows = pl.ds(r0, chunk)
        hs = range(n_heads)
        w = w_ref[...]
        lv = lv_ref[...]
        q, k, vb, la, gate = zip(*[gates(rows, hh) for hh in hs])
        hl = [_split(la[hh]) for hh in hs]
        ex = [jnp.exp(_dot(w, hl[hh][0]) + _dot(w, hl[hh][1])) for hh in hs]
        p = [[_dot_nt(q[hh].astype(BF16), k[hh].astype(BF16))] for hh in hs]
        for l in range(levels):
            for hh in hs:
                xl = ex[hh][l * chunk:(l + 1) * chunk]
                p[hh].append(_dot_nt((q[hh] * xl).astype(BF16), (k[hh] * xl).astype(BF16)))
        att = []
        for hh in hs:
            a = jnp.where(lv == levels, p[hh][0], 0.0)
            for l in range(levels):
                a = jnp.where(lv == l, p[hh][l + 1], a)
            att.append(a.astype(BF16))
        qt = [q[hh] * ex[hh][levels * chunk:(levels + 1) * chunk] for hh in hs]
        kt = [k[hh] * ex[hh][(levels + 1) * chunk:] for hh in hs]
        o = [_dot(att[hh], vb[hh]) for hh in hs]
        inter = [[_dot(qt[hh][n * seq:(n + 1) * seq].astype(BF16), states[hh * n_seq + n].astype(BF16))
                  for n in range(n_seq)] for hh in hs]
        new_states = []
        for hh in hs:
            for n in range(n_seq):
                sl = slice(n * seq, (n + 1) * seq)
                h_n, l_n = hl[hh][0][sl], hl[hh][1][sl]
                decay = jnp.exp(_dot_tn(h_n, ones_bf) + _dot_tn(l_n, ones_bf))
                new_states.append(decay * states[hh * n_seq + n] + _dot_tn(kt[hh][sl].astype(BF16), vb[hh][sl]))
        for hh in hs:
            cols = slice(hh * HEAD, (hh + 1) * HEAD)
            oh = o[hh] + (inter[hh][0] if n_seq == 1 else jnp.concatenate(inter[hh], axis=0))
            o_ref[rows, cols] = (_rms_rows(oh) * on_ref[:, cols] * gate[hh]).astype(o_ref.dtype)
        return tuple(new_states)

    def initial(hh, n):
        if s0_ref is None:
            return jnp.zeros((HEAD, HEAD), F32)
        s = s0_ref[n, hh]
        if dk < HEAD:
            s = jnp.concatenate([s, jnp.zeros((HEAD - dk, HEAD), F32)], axis=0)
        return s

    if carry:
        @pl.when(step == 0)
        def _():
            for hh in range(n_heads):
                st_ref[hh] = initial(hh, 0)
        states = tuple(st_ref[hh] for hh in range(n_heads))
    else:
        states = tuple(initial(hh, n) for hh in range(n_heads) for n in range(n_seq))
    if n_chunks == 1:
        states = do_chunk(0, states)
    else:
        states = lax.fori_loop(
            0, n_chunks, lambda c, st: do_chunk(pl.multiple_of(c * chunk, chunk), st), states)
    if carry:
        for hh in range(n_heads):
            st_ref[hh] = states[hh]

        @pl.when(step == pl.num_programs(1) - 1)
        def _():
            for hh in range(n_heads):
                s_ref[0, hh] = states[hh][:dk]
    else:
        for hh in range(n_heads):
            for n in range(n_seq):
                s_ref[n, hh] = states[hh * n_seq + n][:dk]


def _recurrence(kind, proj, n_groups, s0, consts, layer_consts, *, chunk, seq, n_chunks, seq_steps, out_dtype):
    w, lv, levels = consts
    rows = chunk * n_chunks
    n_seq = chunk // seq
    dk = HEAD if kind == "hgrn" else GLA_DK
    n_heads = N_A if kind == "hgrn" else N_B
    carry = seq_steps > 1
    assert n_seq == 1 or not carry
    assert s0 is None or not carry

    def col(c0):
        return pl.BlockSpec((rows, n_heads * HEAD), lambda g, t: (g * seq_steps + t, c0 // n_heads))

    def whole(a):
        return pl.BlockSpec(a.shape, lambda g, t: (0,) * a.ndim)

    if kind == "hgrn":
        args = [proj, proj, proj, proj, *layer_consts, w, lv]
        specs = [col(COL_AQ), col(COL_AF), col(COL_AI), col(COL_AG)]
    else:
        args = [proj, proj, proj, proj, *layer_consts, w, lv]
        specs = [col(COL_BQ), col(COL_BK), col(COL_BV), col(COL_BG)]
    specs += [whole(a) for a in args[4:]]
    state_spec = pl.BlockSpec((n_seq, n_heads, dk, HEAD), lambda g, t: (g, 0, 0, 0))
    if s0 is not None:
        args.append(s0)
        specs.append(state_spec)
    kern = functools.partial(_rec_kernel, kind=kind, chunk=chunk, seq=seq, n_chunks=n_chunks, levels=levels,
                             has_s0=s0 is not None, dk=dk, n_heads=n_heads, carry=carry)
    return pl.pallas_call(
        kern,
        grid=(n_groups, seq_steps),
        in_specs=specs,
        out_specs=[pl.BlockSpec((rows, n_heads * HEAD), lambda g, t: (g * seq_steps + t, 0)), state_spec],
        out_shape=[jax.ShapeDtypeStruct((n_groups * seq_steps * rows, n_heads * HEAD), out_dtype),
                   jax.ShapeDtypeStruct((n_groups * n_seq, n_heads, dk, HEAD), F32)],
        scratch_shapes=[pltpu.VMEM((n_heads, HEAD, HEAD), F32)] if carry else [],
        compiler_params=_params(("parallel", "arbitrary")),
        name="rec_" + kind,
    )(*args)


def _tri_suffix(n):
    j = np.arange(n)
    return jnp.asarray(j[:, None] >= j[None, :], BF16)


def _sb_prompt_kernel(bias_ref, q_ref, k_ref, v_ref, on_ref, tri_ref, o_ref, *, blk, scale, n_heads):
    h0 = pl.program_id(1) * n_heads
    qb = pl.program_id(2)
    tri = tri_ref[...]
    cols = [slice(i * HEAD, (i + 1) * HEAD) for i in range(n_heads)]
    q = [q_ref[:, c].astype(BF16) for c in cols]
    bias = [bias_ref[h0 + i] for i in range(n_heads)]

    def block(kb, masked, carry):
        rows = pl.ds(pl.multiple_of(kb * blk, blk), blk)
        hs = range(n_heads)
        z = [_dot_nt(q[i], k_ref[rows, cols[i]].astype(BF16)) * scale + bias[i] for i in hs]
        sp = [_softplus(z[i]) for i in hs]
        if masked:
            vis = lax.broadcasted_iota(jnp.int32, (blk, blk), 1) < lax.broadcasted_iota(jnp.int32, (blk, blk), 0)
            sp = [jnp.where(vis, s, 0.0) for s in sp]
        cs = [_dot(jnp.concatenate(_split(sp[i]), axis=1), tri) for i in hs]
        w = [jnp.exp(z[i] - (cs[i] + carry[i][1])) for i in hs]
        if masked:
            w = [jnp.where(vis, x, 0.0) for x in w]
        acc = [carry[i][0] + _dot(w[i].astype(BF16), v_ref[rows, cols[i]].astype(BF16)) for i in hs]
        return tuple((acc[i], carry[i][1] + cs[i][:, 0:1]) for i in hs)

    zero = (jnp.zeros((blk, HEAD), F32), jnp.zeros((blk, 1), F32))
    carry = block(qb, True, (zero,) * n_heads)
    carry = lax.fori_loop(1, qb + 1, lambda j, c: block(qb - j, False, c), carry)
    for i in range(n_heads):
        o_ref[:, cols[i]] = (_rms_rows(carry[i][0]) * on_ref[:, cols[i]]).astype(o_ref.dtype)


def _sb_prompt(proj, bias, onorm, n_batch, t):
    blk = ATT_BLOCK
    nq = t // blk
    nh = ATT_HEADS_PER_STEP
    tri = _tri_suffix(blk)
    tri = jnp.concatenate([tri, tri], axis=0)
    kern = functools.partial(_sb_prompt_kernel, blk=blk, scale=HEAD ** -0.5, n_heads=nh)
    return pl.pallas_call(
        kern,
        grid=(n_batch, N_C // nh, nq),
        in_specs=[
            pl.BlockSpec(memory_space=pltpu.SMEM),
            pl.BlockSpec((blk, nh * HEAD), lambda b, h, i: (b * nq + i, COL_CQ // nh + h)),
            pl.BlockSpec((t, nh * HEAD), lambda b, h, i: (b, COL_CK // nh + h)),
            pl.BlockSpec((t, nh * HEAD), lambda b, h, i: (b, COL_CV // nh + h)),
            pl.BlockSpec((1, nh * HEAD), lambda b, h, i: (0, h)),
            pl.BlockSpec((2 * blk, blk), lambda b, h, i: (0, 0)),
        ],
        out_specs=pl.BlockSpec((blk, nh * HEAD), lambda b, h, i: (b * nq + i, h)),
        out_shape=jax.ShapeDtypeStruct((n_batch * t, N_C * HEAD), BF16),
        compiler_params=_params(("parallel", "parallel", "arbitrary")),
        name="sb_prompt",
    )(bias, proj, proj, proj, onorm, tri)


def _sb_sample_kernel(pt_ref, *refs, n_new, scale):
    del pt_ref
    npg = PAGES_PER_STEP
    q_ref, kn_ref, vn_ref, bias_ref, on_ref, tri_ref = refs[:6]
    k_refs = refs[6:6 + npg]
    v_refs = refs[6 + npg:6 + 2 * npg]
    o_ref, acc_ref, run_ref = refs[6 + 2 * npg:]
    j = pl.program_id(1)
    rows = N_C * n_new
    n_pairs = N_C // 2
    bias = bias_ref[...]
    zq = jnp.zeros((n_new, HEAD), F32)
    q2 = []
    for g in range(n_pairs):
        qa = q_ref[:, (2 * g) * HEAD:(2 * g + 1) * HEAD]
        qb = q_ref[:, (2 * g + 1) * HEAD:(2 * g + 2) * HEAD]
        q2.append(jnp.concatenate([jnp.concatenate([qa, zq], axis=1),
                                   jnp.concatenate([zq, qb], axis=1)], axis=0).astype(BF16))

    def attend(k_blocks, v_blocks, vis):
        nb = len(k_blocks)
        z = [jnp.concatenate([_dot_nt(q2[g], k_blocks[b][g]) for g in range(n_pairs)], axis=0) * scale + bias
             for b in range(nb)]
        group = 2 if nb % 2 == 0 else 1
        res = []
        for b0 in range(0, nb, group):
            parts = []
            for b in range(b0, b0 + group):
                sp = _softplus(z[b])
                if vis is not None:
                    sp = jnp.where(vis, sp, 0.0)
                parts += list(_split(sp))
            res.append(_dot(jnp.concatenate(parts, axis=0), tri_ref[...]))
        res = jnp.concatenate(res, axis=0) if len(res) > 1 else res[0]
        run = run_ref[...]
        w = []
        for b in range(nb):
            r = res[2 * b * rows:(2 * b + 1) * rows] + res[(2 * b + 1) * rows:(2 * b + 2) * rows]
            wb = jnp.exp(z[b] - (r[:, :PAGE] + run))
            if vis is not None:
                wb = jnp.where(vis, wb, 0.0)
            w.append(wb)
            run = run + r[:, PAGE:]
        run_ref[...] = run
        for b0 in range(0, nb, group):
            for g in range(n_pairs):
                sl = slice(2 * g * n_new, (2 * g + 2) * n_new)
                wg = jnp.concatenate([w[b][sl] for b in range(b0, b0 + group)], axis=1).astype(BF16)
                vg = v_blocks[b0][g] if group == 1 else jnp.concatenate(
                    [v_blocks[b][g] for b in range(b0, b0 + group)], axis=0)
                r = _dot(wg, vg)
                acc_ref[2 * g * n_new:(2 * g + 1) * n_new, :] += r[:n_new, :HEAD]
                acc_ref[(2 * g + 1) * n_new:(2 * g + 2) * n_new, :] += r[n_new:, HEAD:]

    @pl.when(j == 0)
    def _():
        acc_ref[...] = jnp.zeros_like(acc_ref)
        run_ref[...] = jnp.zeros_like(run_ref)
        pad = jnp.zeros((PAGE - n_new, 2 * HEAD), F32)
        pair = lambda ref, g: jnp.concatenate([ref[:, 2 * g * HEAD:(2 * g + 2) * HEAD], pad], axis=0).astype(BF16)
        s_idx = lax.broadcasted_iota(jnp.int32, (rows, PAGE), 1)
        i_idx = lax.broadcasted_iota(jnp.int32, (rows, PAGE), 0) % n_new
        attend([[pair(kn_ref, g) for g in range(n_pairs)]], [[pair(vn_ref, g) for g in range(n_pairs)]], s_idx < i_idx)

    def head_pair(ref, g):
        return jnp.concatenate([ref[pl.ds(2 * g, PAGE, stride=N_C), :], ref[pl.ds(2 * g + 1, PAGE, stride=N_C), :]],
                               axis=1).astype(BF16)

    attend([[head_pair(k_refs[p], g) for g in range(n_pairs)] for p in range(npg)],
           [[head_pair(v_refs[p], g) for g in range(n_pairs)] for p in range(npg)], None)

    @pl.when(j == pl.num_programs(1) - 1)
    def _():
        for h in range(N_C):
            sl = slice(h * HEAD, (h + 1) * HEAD)
            o_ref[:, sl] = _rms_rows(acc_ref[h * n_new:(h + 1) * n_new, :]) * on_ref[:, sl]


def _sb_sample(proj, cache_k, cache_v, layer, page_idx, bias_rows, onorm, n_batch, n_new, n_pages):
    npg = PAGES_PER_STEP
    steps = n_pages // npg
    width = N_C * HEAD
    rows = N_C * n_new
    idx = np.arange(PAGE)
    tri = jnp.asarray(np.concatenate([idx[:, None] >= idx[None, :], np.ones((PAGE, PAGE), bool)], axis=1), BF16)

    def page_spec(p):
        return pl.BlockSpec((None, None, PAGE * N_C, HEAD),
                            lambda b, j, pt: (layer, pt[b * n_pages + n_pages - 1 - (j * npg + p)], 0, 0))

    def const(shape):
        return pl.BlockSpec(shape, lambda b, j, pt: (0, 0))

    grid_spec = pltpu.PrefetchScalarGridSpec(
        num_scalar_prefetch=1,
        grid=(n_batch, steps),
        in_specs=[
            pl.BlockSpec((n_new, width), lambda b, j, pt: (b, COL_CQ // N_C)),
            pl.BlockSpec((n_new, width), lambda b, j, pt: (b, COL_CK // N_C)),
            pl.BlockSpec((n_new, width), lambda b, j, pt: (b, COL_CV // N_C)),
            const((rows, PAGE)), const((1, width)), const((PAGE, 2 * PAGE)),
        ] + [page_spec(p) for p in range(npg)] * 2,
        out_specs=pl.BlockSpec((n_new, width), lambda b, j, pt: (b, 0)),
        scratch_shapes=[pltpu.VMEM((rows, HEAD), F32), pltpu.VMEM((rows, PAGE), F32)],
    )
    kern = functools.partial(_sb_sample_kernel, n_new=n_new, scale=HEAD ** -0.5)
    return pl.pallas_call(
        kern,
        grid_spec=grid_spec,
        out_shape=jax.ShapeDtypeStruct((n_batch * n_new, width), F32),
        compiler_params=_params(("parallel", "arbitrary")),
        name="sb_sample",
    )(page_idx, proj, proj, proj, bias_rows, onorm, tri, *([cache_k] * npg), *([cache_v] * npg))


def _prep_w_in(w):
    w = w.astype(BF16)
    a, b_q, b_k, b_v, b_g, b_lr, c = (w[..., :2048], w[..., 2048:2304], w[..., 2304:2560], w[..., 2560:3072],
                                      w[..., 3072:3584], w[..., 3584:3600], w[..., 3600:])
    zq = jnp.zeros(w.shape[:-1] + (HEAD - GLA_DK,), w.dtype)
    bq_parts, bk_parts = [], []
    for h in range(N_B):
        sl = slice(h * GLA_DK, (h + 1) * GLA_DK)
        pad = jnp.concatenate([b_lr, zq[..., b_lr.shape[-1]:]], axis=-1) if h == 0 else zq
        bq_parts += [b_q[..., sl], pad]
        bk_parts += [b_k[..., sl], zq]
    out = jnp.concatenate([c, a] + bq_parts + bk_parts + [b_v, b_g], axis=-1)
    assert out.shape[-1] == PROJ_W
    return out


def _prep_gla_gate(w_up, b_up):
    rank = w_up.shape[0]
    wp = jnp.zeros((HEAD, N_B * HEAD), F32)
    bp = jnp.zeros((1, N_B * HEAD), F32)
    for h in range(N_B):
        wp = wp.at[LR_LANE0:LR_LANE0 + rank, h * HEAD:h * HEAD + GLA_DK].set(w_up[:, h * GLA_DK:(h + 1) * GLA_DK])
        bp = bp.at[0, h * HEAD:h * HEAD + GLA_DK].set(b_up[h * GLA_DK:(h + 1) * GLA_DK])
    return wp.astype(BF16), bp


def _pad_ff(w, axis):
    ff = w.shape[axis]
    ffp = -(-ff // FF_TILE) * FF_TILE
    pad = [(0, 0)] * w.ndim
    pad[axis] = (0, ffp - ff)
    return jnp.pad(w.astype(BF16), pad)


def kernel(x_prompt, x_sample, state_hgrn, state_gla, cache_k, cache_v, page_table, norm_ffn1, w_ffn1_gate, w_ffn1_up, w_ffn1_down, norm_mix, w_in, w_gla_alpha_up, b_gla_alpha, hgrn_lb_logits, qnorm_c, knorm_c, sb_bias, onorm_a, onorm_b, onorm_c, w_out, norm_ffn2, w_ffn2_gate, w_ffn2_up, w_ffn2_down):
    depth = w_in.shape[0]
    bp, tp, d = x_prompt.shape
    bs, ts, _ = x_sample.shape
    n_pages = page_table.shape[1]

    lb_cum = jnp.cumsum(jax.nn.softmax(hgrn_lb_logits.astype(F32), axis=0), axis=0)
    lb_all = lb_cum - lb_cum[0:1]

    xp = x_prompt.reshape(bp * tp, d)
    xs = x_sample.reshape(bs * ts, d)
    pt_flat = page_table.reshape(-1).astype(jnp.int32)
    ck = cache_k.reshape(cache_k.shape[:2] + (PAGE * N_C, HEAD))
    cv = cache_v.reshape(cache_v.shape[:2] + (PAGE * N_C, HEAD))

    ffn1_w = (_pad_ff(w_ffn1_gate, 2), _pad_ff(w_ffn1_up, 2), _pad_ff(w_ffn1_down, 1))
    ffn2_w = (_pad_ff(w_ffn2_gate, 2), _pad_ff(w_ffn2_up, 2), _pad_ff(w_ffn2_down, 1))
    w_in_b = _prep_w_in(w_in)
    w_out_b = w_out.astype(BF16)
    consts_p = _rec_consts(REC_CHUNK, REC_CHUNK)
    consts_s = _rec_consts(SAMPLE_GROUP * ts, ts)

    outs = {k: [] for k in ("hp", "gp", "kp", "vp", "hs", "gs", "ks", "vs")}
    for l in range(depth):
        row = lambda a: a[l].reshape(1, -1).astype(F32)
        ffn1 = (row(norm_ffn1),) + ffn1_w + (l,)
        ffn2 = (row(norm_ffn2),) + ffn2_w + (l,)
        lb = lb_all[l]
        lb3 = jnp.stack([jnp.log(lb), jnp.log1p(-lb), 1.0 - lb])
        hgrn_consts = (lb3, row(onorm_a))
        gla_consts = _prep_gla_gate(w_gla_alpha_up[l].astype(F32), b_gla_alpha[l].astype(F32)) + (row(onorm_b),)
        bias = sb_bias[l].astype(F32)
        on_c = row(onorm_c)

        xp = _ffn(xp, *ffn1)
        pj, kn, vn = _proj(xp, row(norm_mix), w_in_b, row(qnorm_c), row(knorm_c), l)
        rec = dict(chunk=REC_CHUNK, seq=REC_CHUNK, n_chunks=REC_ROWS // REC_CHUNK, seq_steps=tp // REC_ROWS, out_dtype=BF16)
        oa, hp = _recurrence("hgrn", pj, bp, None, consts_p, hgrn_consts, **rec)
        ob, gp = _recurrence("gla", pj, bp, None, consts_p, gla_consts, **rec)
        oc = _sb_prompt(pj, bias, on_c, bp, tp)
        xp = _outproj(xp, oa, ob, oc, w_out_b, l)
        xp = _ffn(xp, *ffn2)
        outs["hp"].append(hp)
        outs["gp"].append(gp)
        outs["kp"].append(kn.reshape(bp, tp, N_C, HEAD))
        outs["vp"].append(vn.reshape(bp, tp, N_C, HEAD))

        xs = _ffn(xs, *ffn1)
        pj, kn, vn = _proj(xs, row(norm_mix), w_in_b, row(qnorm_c), row(knorm_c), l)
        rec = dict(chunk=SAMPLE_GROUP * ts, seq=ts, n_chunks=1, seq_steps=1, out_dtype=BF16)
        oa, hs = _recurrence("hgrn", pj, bs // SAMPLE_GROUP, state_hgrn[l], consts_s, hgrn_consts, **rec)
        ob, gs = _recurrence("gla", pj, bs // SAMPLE_GROUP, state_gla[l], consts_s, gla_consts, **rec)
        bias_rows = jnp.broadcast_to(jnp.repeat(bias, ts)[:, None], (N_C * ts, PAGE))
        oc = _sb_sample(pj, ck, cv, l, pt_flat, bias_rows, on_c, bs, ts, n_pages)
        xs = _outproj(xs, oa, ob, oc, w_out_b, l)
        xs = _ffn(xs, *ffn2)
        outs["hs"].append(hs)
        outs["gs"].append(gs)
        outs["ks"].append(kn.reshape(bs, ts, N_C, HEAD))
        outs["vs"].append(vn.reshape(bs, ts, N_C, HEAD))

    st = lambda k: jnp.stack(outs[k])
    return (xp.reshape(bp, tp, d), xs.reshape(bs, ts, d), st("hp"), st("gp"), st("kp"), st("vp"),
            st("hs"), st("gs"), st("ks"), st("vs"))
```

```python
import functools
import math

import jax
import jax.numpy as jnp
import numpy as np
from jax import lax
from jax.experimental import pallas as pl
from jax.experimental.pallas import tpu as pltpu

F32 = jnp.float32
BF16 = jnp.bfloat16
EPS = 1e-6

LANES = 128
HEAD = 128
GLA_DK = 64
GLA_TAU = 16.0
N_A = 4
N_B = 4
N_C = 8
PAGE = 128
VMEM_LIMIT = 56 * 1024 * 1024

COL_CQ, COL_CK, COL_CV = 0, 8, 16
COL_AQ, COL_AF, COL_AI, COL_AG = 24, 28, 32, 36
COL_BQ, COL_BK, COL_BV, COL_BG = 40, 44, 48, 52
PROJ_W = 56 * LANES
LR_LANE0 = GLA_DK

FF_TILE = 512
ROW_TILE = 512
PROJ_TILE = 1024
ATT_BLOCK = 256
ATT_HEADS_PER_STEP = 4
REC_CHUNK = 64
REC_ROWS = 512
SAMPLE_GROUP = 16
PAGES_PER_STEP = 8


def _params(sem):
    return pltpu.CompilerParams(dimension_semantics=sem, vmem_limit_bytes=VMEM_LIMIT)


def _dot(a, b):
    return jnp.dot(a, b, preferred_element_type=F32)


def _dot_nt(a, b):
    return lax.dot_general(a, b, (((1,), (1,)), ((), ())), preferred_element_type=F32)


def _dot_tn(a, b):
    return lax.dot_general(a, b, (((0,), (0,)), ((), ())), preferred_element_type=F32)


def _split(x):
    hi = x.astype(BF16)
    lo = (x - hi.astype(F32)).astype(BF16)
    return hi, lo


def _log_sigmoid(x):
    return jnp.minimum(x, 0.0) - jnp.log1p(jnp.exp(-jnp.abs(x)))


def _softplus(x):
    return jnp.maximum(x, 0.0) + jnp.log(1.0 + jnp.exp(-jnp.abs(x)))


def _rms_rows(x):
    return x * lax.rsqrt(jnp.mean(x * x, axis=-1, keepdims=True) + EPS)


def _ffn_kernel(x_ref, g_ref, wg_ref, wu_ref, wd_ref, o_ref, h_ref, acc_ref):
    f = pl.program_id(1)

    @pl.when(f == 0)
    def _():
        h_ref[...] = (_rms_rows(x_ref[...]) * g_ref[...]).astype(BF16)
        acc_ref[...] = jnp.zeros_like(acc_ref)

    h = h_ref[...]
    gate = _dot(h, wg_ref[...])
    up = _dot(h, wu_ref[...])
    act = (gate * jax.nn.sigmoid(gate) * up).astype(BF16)
    acc_ref[...] += _dot(act, wd_ref[...])

    @pl.when(f == pl.num_programs(1) - 1)
    def _():
        o_ref[...] = x_ref[...] + 0.5 * acc_ref[...]


def _ffn(x, g, wg, wu, wd, layer):
    t, d = x.shape
    ffp = wg.shape[2]
    return pl.pallas_call(
        _ffn_kernel,
        grid=(t // ROW_TILE, ffp // FF_TILE),
        in_specs=[
            pl.BlockSpec((ROW_TILE, d), lambda i, f: (i, 0)),
            pl.BlockSpec((1, d), lambda i, f: (0, 0)),
            pl.BlockSpec((None, d, FF_TILE), lambda i, f: (layer, 0, f)),
            pl.BlockSpec((None, d, FF_TILE), lambda i, f: (layer, 0, f)),
            pl.BlockSpec((None, FF_TILE, d), lambda i, f: (layer, f, 0)),
        ],
        out_specs=pl.BlockSpec((ROW_TILE, d), lambda i, f: (i, 0)),
        out_shape=jax.ShapeDtypeStruct((t, d), F32),
        scratch_shapes=[pltpu.VMEM((ROW_TILE, d), BF16), pltpu.VMEM((ROW_TILE, d), F32)],
        compiler_params=_params(("parallel", "arbitrary")),
        name="ffn",
    )(x, g, wg, wu, wd)


def _proj_kernel(x_ref, g_ref, w_ref, qn_ref, kn_ref, o_ref, k_ref, v_ref, h_ref):
    j = pl.program_id(1)

    @pl.when(j == 0)
    def _():
        h_ref[...] = (_rms_rows(x_ref[...]) * g_ref[...]).astype(BF16)

    acc = _dot(h_ref[...], w_ref[...])

    @pl.when(j >= 2)
    def _():
        o_ref[...] = acc

    @pl.when(j == 2)
    def _():
        v_ref[...] = acc

    @pl.when(j < 2)
    def _():
        gain = jnp.where(j == 0, qn_ref[...], kn_ref[...])
        for h in range(PROJ_TILE // HEAD):
            sl = slice(h * HEAD, (h + 1) * HEAD)
            o_ref[:, sl] = _rms_rows(acc[:, sl]) * gain

    @pl.when(j == 1)
    def _():
        k_ref[...] = o_ref[...]


def _proj(x, g, w, qn, kn, layer):
    t, d = x.shape
    rows = ROW_TILE
    kv_spec = pl.BlockSpec((rows, PROJ_TILE), lambda i, j: (i, 0))
    kv_shape = jax.ShapeDtypeStruct((t, PROJ_TILE), F32)
    return pl.pallas_call(
        _proj_kernel,
        grid=(t // rows, PROJ_W // PROJ_TILE),
        in_specs=[
            pl.BlockSpec((rows, d), lambda i, j: (i, 0)),
            pl.BlockSpec((1, d), lambda i, j: (0, 0)),
            pl.BlockSpec((None, d, PROJ_TILE), lambda i, j: (layer, 0, j)),
            pl.BlockSpec((1, HEAD), lambda i, j: (0, 0)),
            pl.BlockSpec((1, HEAD), lambda i, j: (0, 0)),
        ],
        out_specs=[pl.BlockSpec((rows, PROJ_TILE), lambda i, j: (i, j)), kv_spec, kv_spec],
        out_shape=[jax.ShapeDtypeStruct((t, PROJ_W), F32), kv_shape, kv_shape],
        scratch_shapes=[pltpu.VMEM((rows, d), BF16)],
        compiler_params=_params(("parallel", "arbitrary")),
        name="proj",
    )(x, g, w, qn, kn)


def _outproj_kernel(x_ref, a_ref, b_ref, c_ref, w_ref, o_ref):
    wa = a_ref.shape[1]
    wb = b_ref.shape[1]
    acc = _dot(a_ref[...].astype(BF16), w_ref[0:wa, :])
    acc += _dot(b_ref[...].astype(BF16), w_ref[wa:wa + wb, :])
    acc += _dot(c_ref[...].astype(BF16), w_ref[wa + wb:, :])
    o_ref[...] = x_ref[...] + acc


def _outproj(x, oa, ob, oc, w, layer):
    t, d = x.shape
    return pl.pallas_call(
        _outproj_kernel,
        grid=(t // ROW_TILE,),
        in_specs=[
            pl.BlockSpec((ROW_TILE, d), lambda i: (i, 0)),
            pl.BlockSpec((ROW_TILE, oa.shape[1]), lambda i: (i, 0)),
            pl.BlockSpec((ROW_TILE, ob.shape[1]), lambda i: (i, 0)),
            pl.BlockSpec((ROW_TILE, oc.shape[1]), lambda i: (i, 0)),
            pl.BlockSpec((None,) + w.shape[1:], lambda i: (layer, 0, 0)),
        ],
        out_specs=pl.BlockSpec((ROW_TILE, d), lambda i: (i, 0)),
        out_shape=jax.ShapeDtypeStruct((t, d), F32),
        compiler_params=_params(("parallel",)),
        name="outproj",
    )(x, oa, ob, oc, w)


def _rec_consts(chunk, seq):
    levels = int(math.log2(seq))
    t = np.arange(chunk)
    col = t[None, :]
    row = t[:, None]
    w = np.zeros((levels + 2, chunk, chunk), np.float32)
    for l in range(levels):
        m = 1 << l
        same = (row // m) == (col // m)
        right = ((row // m) % 2) == 1
        w[l] = np.where(right, same & (col <= row), same & (col > row))
    same_seq = (row // seq) == (col // seq)
    w[levels] = same_seq & (col <= row)
    w[levels + 1] = same_seq & (col > row)
    lv = np.full((chunk, chunk), -1, np.int32)
    strict = same_seq & (col < row)
    x = np.bitwise_xor(row, col)
    lv[strict] = np.floor(np.log2(np.maximum(x, 1)))[strict].astype(np.int32)
    lv[row == col] = levels
    w = w.reshape(-1, chunk)
    return jnp.asarray(np.concatenate([w, w], axis=1), BF16), jnp.asarray(lv), levels


def _rec_kernel(*refs, kind, chunk, seq, n_chunks, levels, has_s0, dk, n_heads, carry):
    refs = list(refs)
    if kind == "hgrn":
        q_ref, f_ref, v_ref, g_ref, lb_ref, on_ref, w_ref, lv_ref = refs[:8]
        rest = refs[8:]
    else:
        q_ref, k_ref, v_ref, g_ref, wup_ref, bup_ref, on_ref, w_ref, lv_ref = refs[:9]
        rest = refs[9:]
    s0_ref = rest.pop(0) if has_s0 else None
    o_ref, s_ref = rest[:2]
    st_ref = rest[2] if carry else None
    n_seq = chunk // seq
    step = pl.program_id(1)
    ones_bf = jnp.ones((2 * seq, HEAD), BF16)

    def gates(rows, hh):
        cols = slice(hh * HEAD, (hh + 1) * HEAD)
        if kind == "hgrn":
            fr = f_ref[rows, cols]
            ls = _log_sigmoid(fr)
            x1 = lb_ref[0:1, cols]
            x2 = lb_ref[1:2, cols] + ls
            la = jnp.maximum(x1, x2) + jnp.log1p(jnp.exp(-jnp.abs(x1 - x2)))
            k = lb_ref[2:3, cols] * jnp.exp(ls - fr)
            q = q_ref[rows, cols]
            gate = jax.nn.sigmoid(g_ref[rows, cols])
        else:
            x = _dot(q_ref[rows, 0:HEAD].astype(BF16), wup_ref[:, cols]) + bup_ref[:, cols]
            la = _log_sigmoid(x) * (1.0 / GLA_TAU)
            k = k_ref[rows, cols]
            q = q_ref[rows, cols] * (GLA_DK ** -0.5)
            g = g_ref[rows, cols]
            gate = g * jax.nn.sigmoid(g)
        return q, k, v_ref[rows, cols].astype(BF16), la, gate

    def do_chunk(r0, states):
        rows = pl.ds(r0, chunk)
        hs = range(n_heads)
        w = w_ref[...]
        lv = lv_ref[...]
        q, k, vb, la, gate = zip(*[gates(rows, hh) for hh in hs])
        hl = [_split(la[hh]) for hh in hs]
        ex = [jnp.exp(_dot(w, jnp.concatenate(hl[hh], axis=0))) for hh in hs]
        p = [[_dot_nt(q[hh].astype(BF16), k[hh].astype(BF16))] for hh in hs]
        for l in range(levels):
            for hh in hs:
                xl = ex[hh][l * chunk:(l + 1) * chunk]
                p[hh].append(_dot_nt((q[hh] * xl).astype(BF16), (k[hh] * xl).astype(BF16)))
        att = []
        for hh in hs:
            a = jnp.where(lv == levels, p[hh][0], 0.0)
            for l in range(levels):
                a = jnp.where(lv == l, p[hh][l + 1], a)
            att.append(a.astype(BF16))
        qt = [q[hh] * ex[hh][levels * chunk:(levels + 1) * chunk] for hh in hs]
        kt = [k[hh] * ex[hh][(levels + 1) * chunk:] for hh in hs]
        o = [_dot(att[hh], vb[hh]) for hh in hs]
        inter = [[_dot(qt[hh][n * seq:(n + 1) * seq].astype(BF16), states[hh * n_seq + n].astype(BF16))
                  for n in range(n_seq)] for hh in hs]
        new_states = []
        for hh in hs:
            for n in range(n_seq):
                sl = slice(n * seq, (n + 1) * seq)
                hl_n = jnp.concatenate([hl[hh][0][sl], hl[hh][1][sl]], axis=0)
                decay = jnp.exp(_dot_tn(hl_n, ones_bf))
                new_states.append(decay * states[hh * n_seq + n] + _dot_tn(kt[hh][sl].astype(BF16), vb[hh][sl]))
        for hh in hs:
            cols = slice(hh * HEAD, (hh + 1) * HEAD)
            oh = o[hh] + (inter[hh][0] if n_seq == 1 else jnp.concatenate(inter[hh], axis=0))
            o_ref[rows, cols] = (_rms_rows(oh) * on_ref[:, cols] * gate[hh]).astype(o_ref.dtype)
        return tuple(new_states)

    def initial(hh, n):
        if s0_ref is None:
            return jnp.zeros((HEAD, HEAD), F32)
        s = s0_ref[n, hh]
        if dk < HEAD:
            s = jnp.concatenate([s, jnp.zeros((HEAD - dk, HEAD), F32)], axis=0)
        return s

    if carry:
        @pl.when(step == 0)
        def _():
            for hh in range(n_heads):
                st_ref[hh] = initial(hh, 0)
        states = tuple(st_ref[hh] for hh in range(n_heads))
    else:
        states = tuple(initial(hh, n) for hh in range(n_heads) for n in range(n_seq))
    if n_chunks == 1:
        states = do_chunk(0, states)
    else:
        states = lax.fori_loop(
            0, n_chunks, lambda c, st: do_chunk(pl.multiple_of(c * chunk, chunk), st), states)
    if carry:
        for hh in range(n_heads):
            st_ref[hh] = states[hh]

        @pl.when(step == pl.num_programs(1) - 1)
        def _():
            for hh in range(n_heads):
                s_ref[0, hh] = states[hh][:dk]
    else:
        for hh in range(n_heads):
            for n in range(n_seq):
                s_ref[n, hh] = states[hh * n_seq + n][:dk]


def _recurrence(kind, proj, n_groups, s0, consts, layer_consts, *, chunk, seq, n_chunks, seq_steps, out_dtype):
    w, lv, levels = consts
    rows = chunk * n_chunks
    n_seq = chunk // seq
    dk = HEAD if kind == "hgrn" else GLA_DK
    n_heads = N_A if kind == "hgrn" else N_B
    carry = seq_steps > 1
    assert n_seq == 1 or not carry
    assert s0 is None or not carry

    def col(c0):
        return pl.BlockSpec((rows, n_heads * HEAD), lambda g, t: (g * seq_steps + t, c0 // n_heads))

    def whole(a):
        return pl.BlockSpec(a.shape, lambda g, t: (0,) * a.ndim)

    if kind == "hgrn":
        args = [proj, proj, proj, proj, *layer_consts, w, lv]
        specs = [col(COL_AQ), col(COL_AF), col(COL_AI), col(COL_AG)]
    else:
        args = [proj, proj, proj, proj, *layer_consts, w, lv]
        specs = [col(COL_BQ), col(COL_BK), col(COL_BV), col(COL_BG)]
    specs += [whole(a) for a in args[4:]]
    state_spec = pl.BlockSpec((n_seq, n_heads, dk, HEAD), lambda g, t: (g, 0, 0, 0))
    if s0 is not None:
        args.append(s0)
        specs.append(state_spec)
    kern = functools.partial(_rec_kernel, kind=kind, chunk=chunk, seq=seq, n_chunks=n_chunks, levels=levels,
                             has_s0=s0 is not None, dk=dk, n_heads=n_heads, carry=carry)
    return pl.pallas_call(
        kern,
        grid=(n_groups, seq_steps),
        in_specs=specs,
        out_specs=[pl.BlockSpec((rows, n_heads * HEAD), lambda g, t: (g * seq_steps + t, 0)), state_spec],
        out_shape=[jax.ShapeDtypeStruct((n_groups * seq_steps * rows, n_heads * HEAD), out_dtype),
                   jax.ShapeDtypeStruct((n_groups * n_seq, n_heads, dk, HEAD), F32)],
        scratch_shapes=[pltpu.VMEM((n_heads, HEAD, HEAD), F32)] if carry else [],
        compiler_params=_params(("parallel", "arbitrary")),
        name="rec_" + kind,
    )(*args)


def _tri_suffix(n):
    j = np.arange(n)
    return jnp.asarray(j[:, None] >= j[None, :], BF16)


def _sb_prompt_kernel(bias_ref, q_ref, k_ref, v_ref, on_ref, tri_ref, o_ref, *, blk, scale, n_heads):
    h0 = pl.program_id(1) * n_heads
    qb = pl.program_id(2)
    tri = tri_ref[...]
    cols = [slice(i * HEAD, (i + 1) * HEAD) for i in range(n_heads)]
    q = [q_ref[:, c].astype(BF16) for c in cols]
    bias = [bias_ref[h0 + i] for i in range(n_heads)]

    def block(kb, masked, carry):
        rows = pl.ds(pl.multiple_of(kb * blk, blk), blk)
        hs = range(n_heads)
        z = [_dot_nt(q[i], k_ref[rows, cols[i]].astype(BF16)) * scale + bias[i] for i in hs]
        sp = [_softplus(z[i]) for i in hs]
        if masked:
            vis = lax.broadcasted_iota(jnp.int32, (blk, blk), 1) < lax.broadcasted_iota(jnp.int32, (blk, blk), 0)
            sp = [jnp.where(vis, s, 0.0) for s in sp]
        cs = [_dot(jnp.concatenate(_split(sp[i]), axis=1), tri) for i in hs]
        w = [jnp.exp(z[i] - (cs[i] + carry[i][1])) for i in hs]
        if masked:
            w = [jnp.where(vis, x, 0.0) for x in w]
        acc = [carry[i][0] + _dot(w[i].astype(BF16), v_ref[rows, cols[i]].astype(BF16)) for i in hs]
        return tuple((acc[i], carry[i][1] + cs[i][:, 0:1]) for i in hs)

    zero = (jnp.zeros((blk, HEAD), F32), jnp.zeros((blk, 1), F32))
    carry = block(qb, True, (zero,) * n_heads)
    carry = lax.fori_loop(1, qb + 1, lambda j, c: block(qb - j, False, c), carry)
    for i in range(n_heads):
        o_ref[:, cols[i]] = (_rms_rows(carry[i][0]) * on_ref[:, cols[i]]).astype(o_ref.dtype)


def _sb_prompt(proj, bias, onorm, n_batch, t):
    blk = ATT_BLOCK
    nq = t // blk
    nh = ATT_HEADS_PER_STEP
    tri = _tri_suffix(blk)
    tri = jnp.concatenate([tri, tri], axis=0)
    kern = functools.partial(_sb_prompt_kernel, blk=blk, scale=HEAD ** -0.5, n_heads=nh)
    return pl.pallas_call(
        kern,
        grid=(n_batch, N_C // nh, nq),
        in_specs=[
            pl.BlockSpec(memory_space=pltpu.SMEM),
            pl.BlockSpec((blk, nh * HEAD), lambda b, h, i: (b * nq + i, COL_CQ // nh + h)),
            pl.BlockSpec((t, nh * HEAD), lambda b, h, i: (b, COL_CK // nh + h)),
            pl.BlockSpec((t, nh * HEAD), lambda b, h, i: (b, COL_CV // nh + h)),
            pl.BlockSpec((1, nh * HEAD), lambda b, h, i: (0, h)),
            pl.BlockSpec((2 * blk, blk), lambda b, h, i: (0, 0)),
        ],
        out_specs=pl.BlockSpec((blk, nh * HEAD), lambda b, h, i: (b * nq + i, h)),
        out_shape=jax.ShapeDtypeStruct((n_batch * t, N_C * HEAD), BF16),
        compiler_params=_params(("parallel", "parallel", "arbitrary")),
        name="sb_prompt",
    )(bias, proj, proj, proj, onorm, tri)


def _sb_sample_kernel(pt_ref, *refs, n_new, scale):
    del pt_ref
    npg = PAGES_PER_STEP
    q_ref, kn_ref, vn_ref, bias_ref, on_ref, tri_ref = refs[:6]
    k_refs = refs[6:6 + npg]
    v_refs = refs[6 + npg:6 + 2 * npg]
    o_ref, acc_ref, run_ref = refs[6 + 2 * npg:]
    j = pl.program_id(1)
    rows = N_C * n_new
    n_pairs = N_C // 2
    bias = bias_ref[...]
    zq = jnp.zeros((n_new, HEAD), F32)
    q2 = []
    for g in range(n_pairs):
        qa = q_ref[:, (2 * g) * HEAD:(2 * g + 1) * HEAD]
        qb = q_ref[:, (2 * g + 1) * HEAD:(2 * g + 2) * HEAD]
        q2.append(jnp.concatenate([jnp.concatenate([qa, zq], axis=1),
                                   jnp.concatenate([zq, qb], axis=1)], axis=0).astype(BF16))

    def attend(k_blocks, v_blocks, vis):
        nb = len(k_blocks)
        z = [jnp.concatenate([_dot_nt(q2[g], k_blocks[b][g]) for g in range(n_pairs)], axis=0) * scale + bias
             for b in range(nb)]
        group = 2 if nb % 2 == 0 else 1
        res = []
        for b0 in range(0, nb, group):
            parts = []
            for b in range(b0, b0 + group):
                sp = _softplus(z[b])
                if vis is not None:
                    sp = jnp.where(vis, sp, 0.0)
                parts += list(_split(sp))
            res.append(_dot(jnp.concatenate(parts, axis=0), tri_ref[...]))
        res = jnp.concatenate(res, axis=0) if len(res) > 1 else res[0]
        run = run_ref[...]
        w = []
        for b in range(nb):
            r = res[2 * b * rows:(2 * b + 1) * rows] + res[(2 * b + 1) * rows:(2 * b + 2) * rows]
            wb = jnp.exp(z[b] - (r[:, :PAGE] + run))
            if vis is not None:
                wb = jnp.where(vis, wb, 0.0)
            w.append(wb)
            run = run + r[:, PAGE:]
        run_ref[...] = run
        for b0 in range(0, nb, group):
            for g in range(n_pairs):
                sl = slice(2 * g * n_new, (2 * g + 2) * n_new)
                wg = jnp.concatenate([w[b][sl] for b in range(b0, b0 + group)], axis=1).astype(BF16)
                vg = v_blocks[b0][g] if group == 1 else jnp.concatenate(
                    [v_blocks[b][g] for b in range(b0, b0 + group)], axis=0)
                r = _dot(wg, vg)
                acc_ref[2 * g * n_new:(2 * g + 1) * n_new, :] += r[:n_new, :HEAD]
                acc_ref[(2 * g + 1) * n_new:(2 * g + 2) * n_new, :] += r[n_new:, HEAD:]

    @pl.when(j == 0)
    def _():
        acc_ref[...] = jnp.zeros_like(acc_ref)
        run_ref[...] = jnp.zeros_like(run_ref)
        pad = jnp.zeros((PAGE - n_new, 2 * HEAD), F32)
        pair = lambda ref, g: jnp.concatenate([ref[:, 2 * g * HEAD:(2 * g + 2) * HEAD], pad], axis=0).astype(BF16)
        s_idx = lax.broadcasted_iota(jnp.int32, (rows, PAGE), 1)
        i_idx = lax.broadcasted_iota(jnp.int32, (rows, PAGE), 0) % n_new
        attend([[pair(kn_ref, g) for g in range(n_pairs)]], [[pair(vn_ref, g) for g in range(n_pairs)]], s_idx < i_idx)

    def head_pair(ref, g):
        return jnp.concatenate([ref[pl.ds(2 * g, PAGE, stride=N_C), :], ref[pl.ds(2 * g + 1, PAGE, stride=N_C), :]],
                               axis=1).astype(BF16)

    attend([[head_pair(k_refs[p], g) for g in range(n_pairs)] for p in range(npg)],
           [[head_pair(v_refs[p], g) for g in range(n_pairs)] for p in range(npg)], None)

    @pl.when(j == pl.num_programs(1) - 1)
    def _():
        for h in range(N_C):
            sl = slice(h * HEAD, (h + 1) * HEAD)
            o_ref[:, sl] = _rms_rows(acc_ref[h * n_new:(h + 1) * n_new, :]) * on_ref[:, sl]


def _sb_sample(proj, cache_k, cache_v, layer, page_idx, bias_rows, onorm, n_batch, n_new, n_pages):
    npg = PAGES_PER_STEP
    steps = n_pages // npg
    width = N_C * HEAD
    rows = N_C * n_new
    idx = np.arange(PAGE)
    tri = jnp.asarray(np.concatenate([idx[:, None] >= idx[None, :], np.ones((PAGE, PAGE), bool)], axis=1), BF16)

    def page_spec(p):
        return pl.BlockSpec((None, None, PAGE * N_C, HEAD),
                            lambda b, j, pt: (layer, pt[b * n_pages + n_pages - 1 - (j * npg + p)], 0, 0))

    def const(shape):
        return pl.BlockSpec(shape, lambda b, j, pt: (0, 0))

    grid_spec = pltpu.PrefetchScalarGridSpec(
        num_scalar_prefetch=1,
        grid=(n_batch, steps),
        in_specs=[
            pl.BlockSpec((n_new, width), lambda b, j, pt: (b, COL_CQ // N_C)),
            pl.BlockSpec((n_new, width), lambda b, j, pt: (b, COL_CK // N_C)),
            pl.BlockSpec((n_new, width), lambda b, j, pt: (b, COL_CV // N_C)),
            const((rows, PAGE)), const((1, width)), const((PAGE, 2 * PAGE)),
        ] + [page_spec(p) for p in range(npg)] * 2,
        out_specs=pl.BlockSpec((n_new, width), lambda b, j, pt: (b, 0)),
        scratch_shapes=[pltpu.VMEM((rows, HEAD), F32), pltpu.VMEM((rows, PAGE), F32)],
    )
    kern = functools.partial(_sb_sample_kernel, n_new=n_new, scale=HEAD ** -0.5)
    return pl.pallas_call(
        kern,
        grid_spec=grid_spec,
        out_shape=jax.ShapeDtypeStruct((n_batch * n_new, width), F32),
        compiler_params=_params(("parallel", "arbitrary")),
        name="sb_sample",
    )(page_idx, proj, proj, proj, bias_rows, onorm, tri, *([cache_k] * npg), *([cache_v] * npg))


def _prep_w_in(w):
    w = w.astype(BF16)
    a, b_q, b_k, b_v, b_g, b_lr, c = (w[..., :2048], w[..., 2048:2304], w[..., 2304:2560], w[..., 2560:3072],
                                      w[..., 3072:3584], w[..., 3584:3600], w[..., 3600:])
    zq = jnp.zeros(w.shape[:-1] + (HEAD - GLA_DK,), w.dtype)
    bq_parts, bk_parts = [], []
    for h in range(N_B):
        sl = slice(h * GLA_DK, (h + 1) * GLA_DK)
        pad = jnp.concatenate([b_lr, zq[..., b_lr.shape[-1]:]], axis=-1) if h == 0 else zq
        bq_parts += [b_q[..., sl], pad]
        bk_parts += [b_k[..., sl], zq]
    out = jnp.concatenate([c, a] + bq_parts + bk_parts + [b_v, b_g], axis=-1)
    assert out.shape[-1] == PROJ_W
    return out


def _prep_gla_gate(w_up, b_up):
    rank = w_up.shape[0]
    wp = jnp.zeros((HEAD, N_B * HEAD), F32)
    bp = jnp.zeros((1, N_B * HEAD), F32)
    for h in range(N_B):
        wp = wp.at[LR_LANE0:LR_LANE0 + rank, h * HEAD:h * HEAD + GLA_DK].set(w_up[:, h * GLA_DK:(h + 1) * GLA_DK])
        bp = bp.at[0, h * HEAD:h * HEAD + GLA_DK].set(b_up[h * GLA_DK:(h + 1) * GLA_DK])
    return wp.astype(BF16), bp


def _pad_ff(w, axis):
    ff = w.shape[axis]
    ffp = -(-ff // FF_TILE) * FF_TILE
    pad = [(0, 0)] * w.ndim
    pad[axis] = (0, ffp - ff)
    return jnp.pad(w.astype(BF16), pad)


def kernel(x_prompt, x_sample, state_hgrn, state_gla, cache_k, cache_v, page_table, norm_ffn1, w_ffn1_gate, w_ffn1_up, w_ffn1_down, norm_mix, w_in, w_gla_alpha_up, b_gla_alpha, hgrn_lb_logits, qnorm_c, knorm_c, sb_bias, onorm_a, onorm_b, onorm_c, w_out, norm_ffn2, w_ffn2_gate, w_ffn2_up, w_ffn2_down):
    depth = w_in.shape[0]
    bp, tp, d = x_prompt.shape
    bs, ts, _ = x_sample.shape
    n_pages = page_table.shape[1]

    lb_cum = jnp.cumsum(jax.nn.softmax(hgrn_lb_logits.astype(F32), axis=0), axis=0)
    lb_all = lb_cum - lb_cum[0:1]

    xp = x_prompt.reshape(bp * tp, d)
    xs = x_sample.reshape(bs * ts, d)
    pt_flat = page_table.reshape(-1).astype(jnp.int32)
    ck = cache_k.reshape(cache_k.shape[:2] + (PAGE * N_C, HEAD))
    cv = cache_v.reshape(cache_v.shape[:2] + (PAGE * N_C, HEAD))

    ffn1_w = (_pad_ff(w_ffn1_gate, 2), _pad_ff(w_ffn1_up, 2), _pad_ff(w_ffn1_down, 1))
    ffn2_w = (_pad_ff(w_ffn2_gate, 2), _pad_ff(w_ffn2_up, 2), _pad_ff(w_ffn2_down, 1))
    w_in_b = _prep_w_in(w_in)
    w_out_b = w_out.astype(BF16)
    consts_p = _rec_consts(REC_CHUNK, REC_CHUNK)
    consts_s = _rec_consts(SAMPLE_GROUP * ts, ts)

    outs = {k: [] for k in ("hp", "gp", "kp", "vp", "hs", "gs", "ks", "vs")}
    for l in range(depth):
        row = lambda a: a[l].reshape(1, -1).astype(F32)
        ffn1 = (row(norm_ffn1),) + ffn1_w + (l,)
        ffn2 = (row(norm_ffn2),) + ffn2_w + (l,)
        lb = lb_all[l]
        lb3 = jnp.stack([jnp.log(lb), jnp.log1p(-lb), 1.0 - lb])
        hgrn_consts = (lb3, row(onorm_a))
        gla_consts = _prep_gla_gate(w_gla_alpha_up[l].astype(F32), b_gla_alpha[l].astype(F32)) + (row(onorm_b),)
        bias = sb_bias[l].astype(F32)
        on_c = row(onorm_c)

        xp = _ffn(xp, *ffn1)
        pj, kn, vn = _proj(xp, row(norm_mix), w_in_b, row(qnorm_c), row(knorm_c), l)
        rec = dict(chunk=REC_CHUNK, seq=REC_CHUNK, n_chunks=REC_ROWS // REC_CHUNK, seq_steps=tp // REC_ROWS, out_dtype=BF16)
        oa, hp = _recurrence("hgrn", pj, bp, None, consts_p, hgrn_consts, **rec)
        ob, gp = _recurrence("gla", pj, bp, None, consts_p, gla_consts, **rec)
        oc = _sb_prompt(pj, bias, on_c, bp, tp)
        xp = _outproj(xp, oa, ob, oc, w_out_b, l)
        xp = _ffn(xp, *ffn2)
        outs["hp"].append(hp)
        outs["gp"].append(gp)
        outs["kp"].append(kn.reshape(bp, tp, N_C, HEAD))
        outs["vp"].append(vn.reshape(bp, tp, N_C, HEAD))

        xs = _ffn(xs, *ffn1)
        pj, kn, vn = _proj(xs, row(norm_mix), w_in_b, row(qnorm_c), row(knorm_c), l)
        rec = dict(chunk=SAMPLE_GROUP * ts, seq=ts, n_chunks=1, seq_steps=1, out_dtype=BF16)
        oa, hs = _recurrence("hgrn", pj, bs // SAMPLE_GROUP, state_hgrn[l], consts_s, hgrn_consts, **rec)
        ob, gs = _recurrence("gla", pj, bs // SAMPLE_GROUP, state_gla[l], consts_s, gla_consts, **rec)
        bias_rows = jnp.broadcast_to(jnp.repeat(bias, ts)[:, None], (N_C * ts, PAGE))
        oc = _sb_sample(pj, ck, cv, l, pt_flat, bias_rows, on_c, bs, ts, n_pages)
        xs = _outproj(xs, oa, ob, oc, w_out_b, l)
        xs = _ffn(xs, *ffn2)
        outs["hs"].append(hs)
        outs["gs"].append(gs)
        outs["ks"].append(kn.reshape(bs, ts, N_C, HEAD))
        outs["vs"].append(vn.reshape(bs, ts, N_C, HEAD))

    st = lambda k: jnp.stack(outs[k])
    return (xp.reshape(bp, tp, d), xs.reshape(bs, ts, d), st("hp"), st("gp"), st("kp"), st("vp"),
            st("hs"), st("gs"), st("ks"), st("vs"))
```

```python
import functools
import math

import jax
import jax.numpy as jnp
import numpy as np
from jax import lax
from jax.experimental import pallas as pl
from jax.experimental.pallas import tpu as pltpu

F32 = jnp.float32
BF16 = jnp.bfloat16
EPS = 1e-6

LANES = 128
HEAD = 128
GLA_DK = 64
GLA_TAU = 16.0
N_A = 4
N_B = 4
N_C = 8
PAGE = 128
VMEM_LIMIT = 56 * 1024 * 1024

COL_CQ, COL_CK, COL_CV = 0, 8, 16
COL_AQ, COL_AF, COL_AI, COL_AG = 24, 28, 32, 36
COL_BQ, COL_BK, COL_BV, COL_BG = 40, 44, 48, 52
PROJ_W = 56 * LANES
LR_LANE0 = GLA_DK

FF_TILE = 512
ROW_TILE = 512
PROJ_TILE = 1024
ATT_BLOCK = 256
ATT_HEADS_PER_STEP = 4
REC_CHUNK = 64
REC_ROWS = 512
SAMPLE_GROUP = 16
PAGES_PER_STEP = 16


def _params(sem):
    return pltpu.CompilerParams(dimension_semantics=sem, vmem_limit_bytes=VMEM_LIMIT)


def _dot(a, b):
    return jnp.dot(a, b, preferred_element_type=F32)


def _dot_nt(a, b):
    return lax.dot_general(a, b, (((1,), (1,)), ((), ())), preferred_element_type=F32)


def _dot_tn(a, b):
    return lax.dot_general(a, b, (((0,), (0,)), ((), ())), preferred_element_type=F32)


def _split(x):
    hi = x.astype(BF16)
    lo = (x - hi.astype(F32)).astype(BF16)
    return hi, lo


def _log_sigmoid(x):
    return jnp.minimum(x, 0.0) - jnp.log1p(jnp.exp(-jnp.abs(x)))


def _softplus(x):
    return jnp.maximum(x, 0.0) + jnp.log(1.0 + jnp.exp(-jnp.abs(x)))


def _rms_rows(x):
    return x * lax.rsqrt(jnp.mean(x * x, axis=-1, keepdims=True) + EPS)


def _ffn_kernel(x_ref, g_ref, wg_ref, wu_ref, wd_ref, o_ref, h_ref, acc_ref):
    f = pl.program_id(1)

    @pl.when(f == 0)
    def _():
        h_ref[...] = (_rms_rows(x_ref[...]) * g_ref[...]).astype(BF16)
        acc_ref[...] = jnp.zeros_like(acc_ref)

    h = h_ref[...]
    gate = _dot(h, wg_ref[...])
    up = _dot(h, wu_ref[...])
    act = (gate * jax.nn.sigmoid(gate) * up).astype(BF16)
    acc_ref[...] += _dot(act, wd_ref[...])

    @pl.when(f == pl.num_programs(1) - 1)
    def _():
        o_ref[...] = x_ref[...] + 0.5 * acc_ref[...]


def _ffn(x, g, wg, wu, wd, layer):
    t, d = x.shape
    ffp = wg.shape[2]
    return pl.pallas_call(
        _ffn_kernel,
        grid=(t // ROW_TILE, ffp // FF_TILE),
        in_specs=[
            pl.BlockSpec((ROW_TILE, d), lambda i, f: (i, 0)),
            pl.BlockSpec((1, d), lambda i, f: (0, 0)),
            pl.BlockSpec((None, d, FF_TILE), lambda i, f: (layer, 0, f)),
            pl.BlockSpec((None, d, FF_TILE), lambda i, f: (layer, 0, f)),
            pl.BlockSpec((None, FF_TILE, d), lambda i, f: (layer, f, 0)),
        ],
        out_specs=pl.BlockSpec((ROW_TILE, d), lambda i, f: (i, 0)),
        out_shape=jax.ShapeDtypeStruct((t, d), F32),
        scratch_shapes=[pltpu.VMEM((ROW_TILE, d), BF16), pltpu.VMEM((ROW_TILE, d), F32)],
        compiler_params=_params(("parallel", "arbitrary")),
        name="ffn",
    )(x, g, wg, wu, wd)


def _proj_kernel(x_ref, g_ref, w_ref, qn_ref, kn_ref, o_ref, k_ref, v_ref, h_ref):
    j = pl.program_id(1)

    @pl.when(j == 0)
    def _():
        h_ref[...] = (_rms_rows(x_ref[...]) * g_ref[...]).astype(BF16)

    acc = _dot(h_ref[...], w_ref[...])

    @pl.when(j >= 2)
    def _():
        o_ref[...] = acc

    @pl.when(j == 2)
    def _():
        v_ref[...] = acc

    @pl.when(j < 2)
    def _():
        gain = jnp.where(j == 0, qn_ref[...], kn_ref[...])
        for h in range(PROJ_TILE // HEAD):
            sl = slice(h * HEAD, (h + 1) * HEAD)
            o_ref[:, sl] = _rms_rows(acc[:, sl]) * gain

    @pl.when(j == 1)
    def _():
        k_ref[...] = o_ref[...]


def _proj(x, g, w, qn, kn, layer):
    t, d = x.shape
    rows = ROW_TILE
    kv_spec = pl.BlockSpec((rows, PROJ_TILE), lambda i, j: (i, 0))
    kv_shape = jax.ShapeDtypeStruct((t, PROJ_TILE), F32)
    return pl.pallas_call(
        _proj_kernel,
        grid=(t // rows, PROJ_W // PROJ_TILE),
        in_specs=[
            pl.BlockSpec((rows, d), lambda i, j: (i, 0)),
            pl.BlockSpec((1, d), lambda i, j: (0, 0)),
            pl.BlockSpec((None, d, PROJ_TILE), lambda i, j: (layer, 0, j)),
            pl.BlockSpec((1, HEAD), lambda i, j: (0, 0)),
            pl.BlockSpec((1, HEAD), lambda i, j: (0, 0)),
        ],
        out_specs=[pl.BlockSpec((rows, PROJ_TILE), lambda i, j: (i, j)), kv_spec, kv_spec],
        out_shape=[jax.ShapeDtypeStruct((t, PROJ_W), F32), kv_shape, kv_shape],
        scratch_shapes=[pltpu.VMEM((rows, d), BF16)],
        compiler_params=_params(("parallel", "arbitrary")),
        name="proj",
    )(x, g, w, qn, kn)


def _outproj_kernel(x_ref, a_ref, b_ref, c_ref, w_ref, o_ref):
    wa = a_ref.shape[1]
    wb = b_ref.shape[1]
    acc = _dot(a_ref[...].astype(BF16), w_ref[0:wa, :])
    acc += _dot(b_ref[...].astype(BF16), w_ref[wa:wa + wb, :])
    acc += _dot(c_ref[...].astype(BF16), w_ref[wa + wb:, :])
    o_ref[...] = x_ref[...] + acc


def _outproj(x, oa, ob, oc, w, layer):
    t, d = x.shape
    return pl.pallas_call(
        _outproj_kernel,
        grid=(t // ROW_TILE,),
        in_specs=[
            pl.BlockSpec((ROW_TILE, d), lambda i: (i, 0)),
            pl.BlockSpec((ROW_TILE, oa.shape[1]), lambda i: (i, 0)),
            pl.BlockSpec((ROW_TILE, ob.shape[1]), lambda i: (i, 0)),
            pl.BlockSpec((ROW_TILE, oc.shape[1]), lambda i: (i, 0)),
            pl.BlockSpec((None,) + w.shape[1:], lambda i: (layer, 0, 0)),
        ],
        out_specs=pl.BlockSpec((ROW_TILE, d), lambda i: (i, 0)),
        out_shape=jax.ShapeDtypeStruct((t, d), F32),
        compiler_params=_params(("parallel",)),
        name="outproj",
    )(x, oa, ob, oc, w)


def _rec_consts(chunk, seq):
    levels = int(math.log2(seq))
    t = np.arange(chunk)
    col = t[None, :]
    row = t[:, None]
    w = np.zeros((levels + 2, chunk, chunk), np.float32)
    for l in range(levels):
        m = 1 << l
        same = (row // m) == (col // m)
        right = ((row // m) % 2) == 1
        w[l] = np.where(right, same & (col <= row), same & (col > row))
    same_seq = (row // seq) == (col // seq)
    w[levels] = same_seq & (col <= row)
    w[levels + 1] = same_seq & (col > row)
    lv = np.full((chunk, chunk), -1, np.int32)
    strict = same_seq & (col < row)
    x = np.bitwise_xor(row, col)
    lv[strict] = np.floor(np.log2(np.maximum(x, 1)))[strict].astype(np.int32)
    lv[row == col] = levels
    w = w.reshape(-1, chunk)
    return jnp.asarray(np.concatenate([w, w], axis=1), BF16), jnp.asarray(lv), levels


def _rec_kernel(*refs, kind, chunk, seq, n_chunks, levels, has_s0, dk, n_heads, carry):
    refs = list(refs)
    if kind == "hgrn":
        q_ref, f_ref, v_ref, g_ref, lb_ref, on_ref, w_ref, lv_ref = refs[:8]
        rest = refs[8:]
    else:
        q_ref, k_ref, v_ref, g_ref, wup_ref, bup_ref, on_ref, w_ref, lv_ref = refs[:9]
        rest = refs[9:]
    s0_ref = rest.pop(0) if has_s0 else None
    o_ref, s_ref = rest[:2]
    st_ref = rest[2] if carry else None
    n_seq = chunk // seq
    step = pl.program_id(1)
    ones_bf = jnp.ones((2 * seq, HEAD), BF16)

    def gates(rows, hh):
        cols = slice(hh * HEAD, (hh + 1) * HEAD)
        if kind == "hgrn":
            fr = f_ref[rows, cols]
            ls = _log_sigmoid(fr)
            x1 = lb_ref[0:1, cols]
            x2 = lb_ref[1:2, cols] + ls
            la = jnp.maximum(x1, x2) + jnp.log1p(jnp.exp(-jnp.abs(x1 - x2)))
            k = lb_ref[2:3, cols] * jnp.exp(ls - fr)
            q = q_ref[rows, cols]
            gate = jax.nn.sigmoid(g_ref[rows, cols])
        else:
            x = _dot(q_ref[rows, 0:HEAD].astype(BF16), wup_ref[:, cols]) + bup_ref[:, cols]
            la = _log_sigmoid(x) * (1.0 / GLA_TAU)
            k = k_ref[rows, cols]
            q = q_ref[rows, cols] * (GLA_DK ** -0.5)
            g = g_ref[rows, cols]
            gate = g * jax.nn.sigmoid(g)
        return q, k, v_ref[rows, cols].astype(BF16), la, gate

    def do_chunk(r0, states):
        rows = pl.ds(r0, chunk)
        hs = range(n_heads)
        w = w_ref[...]
        lv = lv_ref[...]
        q, k, vb, la, gate = zip(*[gates(rows, hh) for hh in hs])
        hl = [_split(la[hh]) for hh in hs]
        ex = [jnp.exp(_dot(w, jnp.concatenate(hl[hh], axis=0))) for hh in hs]
        p = [[_dot_nt(q[hh].astype(BF16), k[hh].astype(BF16))] for hh in hs]
        for l in range(levels):
            for hh in hs:
                xl = ex[hh][l * chunk:(l + 1) * chunk]
                p[hh].append(_dot_nt((q[hh] * xl).astype(BF16), (k[hh] * xl).astype(BF16)))
        att = []
        for hh in hs:
            a = jnp.where(lv == levels, p[hh][0], 0.0)
            for l in range(levels):
                a = jnp.where(lv == l, p[hh][l + 1], a)
            att.append(a.astype(BF16))
        qt = [q[hh] * ex[hh][levels * chunk:(levels + 1) * chunk] for hh in hs]
        kt = [k[hh] * ex[hh][(levels + 1) * chunk:] for hh in hs]
        o = [_dot(att[hh], vb[hh]) for hh in hs]
        inter = [[_dot(qt[hh][n * seq:(n + 1) * seq].astype(BF16), states[hh * n_seq + n].astype(BF16))
                  for n in range(n_seq)] for hh in hs]
        new_states = []
        for hh in hs:
            for n in range(n_seq):
                sl = slice(n * seq, (n + 1) * seq)
                hl_n = jnp.concatenate([hl[hh][0][sl], hl[hh][1][sl]], axis=0)
                decay = jnp.exp(_dot_tn(hl_n, ones_bf))
                new_states.append(decay * states[hh * n_seq + n] + _dot_tn(kt[hh][sl].astype(BF16), vb[hh][sl]))
        for hh in hs:
            cols = slice(hh * HEAD, (hh + 1) * HEAD)
            oh = o[hh] + (inter[hh][0] if n_seq == 1 else jnp.concatenate(inter[hh], axis=0))
            o_ref[rows, cols] = (_rms_rows(oh) * on_ref[:, cols] * gate[hh]).astype(o_ref.dtype)
        return tuple(new_states)

    def initial(hh, n):
        if s0_ref is None:
            return jnp.zeros((HEAD, HEAD), F32)
        s = s0_ref[n, hh]
        if dk < HEAD:
            s = jnp.concatenate([s, jnp.zeros((HEAD - dk, HEAD), F32)], axis=0)
        return s

    if carry:
        @pl.when(step == 0)
        def _():
            for hh in range(n_heads):
                st_ref[hh] = initial(hh, 0)
        states = tuple(st_ref[hh] for hh in range(n_heads))
    else:
        states = tuple(initial(hh, n) for hh in range(n_heads) for n in range(n_seq))
    if n_chunks == 1:
        states = do_chunk(0, states)
    else:
        states = lax.fori_loop(
            0, n_chunks, lambda c, st: do_chunk(pl.multiple_of(c * chunk, chunk), st), states)
    if carry:
        for hh in range(n_heads):
            st_ref[hh] = states[hh]

        @pl.when(step == pl.num_programs(1) - 1)
        def _():
            for hh in range(n_heads):
                s_ref[0, hh] = states[hh][:dk]
    else:
        for hh in range(n_heads):
            for n in range(n_seq):
                s_ref[n, hh] = states[hh * n_seq + n][:dk]


def _recurrence(kind, proj, n_groups, s0, consts, layer_consts, *, chunk, seq, n_chunks, seq_steps, out_dtype):
    w, lv, levels = consts
    rows = chunk * n_chunks
    n_seq = chunk // seq
    dk = HEAD if kind == "hgrn" else GLA_DK
    n_heads = N_A if kind == "hgrn" else N_B
    carry = seq_steps > 1
    assert n_seq == 1 or not carry
    assert s0 is None or not carry

    def col(c0):
        return pl.BlockSpec((rows, n_heads * HEAD), lambda g, t: (g * seq_steps + t, c0 // n_heads))

    def whole(a):
        return pl.BlockSpec(a.shape, lambda g, t: (0,) * a.ndim)

    if kind == "hgrn":
        args = [proj, proj, proj, proj, *layer_consts, w, lv]
        specs = [col(COL_AQ), col(COL_AF), col(COL_AI), col(COL_AG)]
    else:
        args = [proj, proj, proj, proj, *layer_consts, w, lv]
        specs = [col(COL_BQ), col(COL_BK), col(COL_BV), col(COL_BG)]
    specs += [whole(a) for a in args[4:]]
    state_spec = pl.BlockSpec((n_seq, n_heads, dk, HEAD), lambda g, t: (g, 0, 0, 0))
    if s0 is not None:
        args.append(s0)
        specs.append(state_spec)
    kern = functools.partial(_rec_kernel, kind=kind, chunk=chunk, seq=seq, n_chunks=n_chunks, levels=levels,
                             has_s0=s0 is not None, dk=dk, n_heads=n_heads, carry=carry)
    return pl.pallas_call(
        kern,
        grid=(n_groups, seq_steps),
        in_specs=specs,
        out_specs=[pl.BlockSpec((rows, n_heads * HEAD), lambda g, t: (g * seq_steps + t, 0)), state_spec],
        out_shape=[jax.ShapeDtypeStruct((n_groups * seq_steps * rows, n_heads * HEAD), out_dtype),
                   jax.ShapeDtypeStruct((n_groups * n_seq, n_heads, dk, HEAD), F32)],
        scratch_shapes=[pltpu.VMEM((n_heads, HEAD, HEAD), F32)] if carry else [],
        compiler_params=_params(("parallel", "arbitrary")),
        name="rec_" + kind,
    )(*args)


def _tri_suffix(n):
    j = np.arange(n)
    return jnp.asarray(j[:, None] >= j[None, :], BF16)


def _sb_prompt_kernel(bias_ref, q_ref, k_ref, v_ref, on_ref, tri_ref, o_ref, *, blk, scale, n_heads):
    h0 = pl.program_id(1) * n_heads
    qb = pl.program_id(2)
    tri = tri_ref[...]
    cols = [slice(i * HEAD, (i + 1) * HEAD) for i in range(n_heads)]
    q = [q_ref[:, c].astype(BF16) for c in cols]
    bias = [bias_ref[h0 + i] for i in range(n_heads)]

    def block(kb, masked, carry):
        rows = pl.ds(pl.multiple_of(kb * blk, blk), blk)
        hs = range(n_heads)
        z = [_dot_nt(q[i], k_ref[rows, cols[i]].astype(BF16)) * scale + bias[i] for i in hs]
        sp = [_softplus(z[i]) for i in hs]
        if masked:
            vis = lax.broadcasted_iota(jnp.int32, (blk, blk), 1) < lax.broadcasted_iota(jnp.int32, (blk, blk), 0)
            sp = [jnp.where(vis, s, 0.0) for s in sp]
        cs = [_dot(jnp.concatenate(_split(sp[i]), axis=1), tri) for i in hs]
        w = [jnp.exp(z[i] - (cs[i] + carry[i][1])) for i in hs]
        if masked:
            w = [jnp.where(vis, x, 0.0) for x in w]
        acc = [carry[i][0] + _dot(w[i].astype(BF16), v_ref[rows, cols[i]].astype(BF16)) for i in hs]
        return tuple((acc[i], carry[i][1] + cs[i][:, 0:1]) for i in hs)

    zero = (jnp.zeros((blk, HEAD), F32), jnp.zeros((blk, 1), F32))
    carry = block(qb, True, (zero,) * n_heads)
    carry = lax.fori_loop(1, qb + 1, lambda j, c: block(qb - j, False, c), carry)
    for i in range(n_heads):
        o_ref[:, cols[i]] = (_rms_rows(carry[i][0]) * on_ref[:, cols[i]]).astype(o_ref.dtype)


def _sb_prompt(proj, bias, onorm, n_batch, t):
    blk = ATT_BLOCK
    nq = t // blk
    nh = ATT_HEADS_PER_STEP
    tri = _tri_suffix(blk)
    tri = jnp.concatenate([tri, tri], axis=0)
    kern = functools.partial(_sb_prompt_kernel, blk=blk, scale=HEAD ** -0.5, n_heads=nh)
    return pl.pallas_call(
        kern,
        grid=(n_batch, N_C // nh, nq),
        in_specs=[
            pl.BlockSpec(memory_space=pltpu.SMEM),
            pl.BlockSpec((blk, nh * HEAD), lambda b, h, i: (b * nq + i, COL_CQ // nh + h)),
            pl.BlockSpec((t, nh * HEAD), lambda b, h, i: (b, COL_CK // nh + h)),
            pl.BlockSpec((t, nh * HEAD), lambda b, h, i: (b, COL_CV // nh + h)),
            pl.BlockSpec((1, nh * HEAD), lambda b, h, i: (0, h)),
            pl.BlockSpec((2 * blk, blk), lambda b, h, i: (0, 0)),
        ],
        out_specs=pl.BlockSpec((blk, nh * HEAD), lambda b, h, i: (b * nq + i, h)),
        out_shape=jax.ShapeDtypeStruct((n_batch * t, N_C * HEAD), BF16),
        compiler_params=_params(("parallel", "parallel", "arbitrary")),
        name="sb_prompt",
    )(bias, proj, proj, proj, onorm, tri)


def _sb_sample_kernel(pt_ref, *refs, n_new, scale):
    del pt_ref
    npg = PAGES_PER_STEP
    q_ref, kn_ref, vn_ref, bias_ref, on_ref, tri_ref = refs[:6]
    k_refs = refs[6:6 + npg]
    v_refs = refs[6 + npg:6 + 2 * npg]
    o_ref, acc_ref, run_ref = refs[6 + 2 * npg:]
    j = pl.program_id(1)
    rows = N_C * n_new
    n_pairs = N_C // 2
    bias = bias_ref[...]
    zq = jnp.zeros((n_new, HEAD), F32)
    q2 = []
    for g in range(n_pairs):
        qa = q_ref[:, (2 * g) * HEAD:(2 * g + 1) * HEAD]
        qb = q_ref[:, (2 * g + 1) * HEAD:(2 * g + 2) * HEAD]
        q2.append(jnp.concatenate([jnp.concatenate([qa, zq], axis=1),
                                   jnp.concatenate([zq, qb], axis=1)], axis=0).astype(BF16))

    def attend(k_blocks, v_blocks, vis):
        nb = len(k_blocks)
        z = [jnp.concatenate([_dot_nt(q2[g], k_blocks[b][g]) for g in range(n_pairs)], axis=0) * scale + bias
             for b in range(nb)]
        group = 2 if nb % 2 == 0 else 1
        res = []
        for b0 in range(0, nb, group):
            parts = []
            for b in range(b0, b0 + group):
                sp = _softplus(z[b])
                if vis is not None:
                    sp = jnp.where(vis, sp, 0.0)
                parts += list(_split(sp))
            res.append(_dot(jnp.concatenate(parts, axis=0), tri_ref[...]))
        res = jnp.concatenate(res, axis=0) if len(res) > 1 else res[0]
        run = run_ref[...]
        w = []
        for b in range(nb):
            r = res[2 * b * rows:(2 * b + 1) * rows] + res[(2 * b + 1) * rows:(2 * b + 2) * rows]
            wb = jnp.exp(z[b] - (r[:, :PAGE] + run))
            if vis is not None:
                wb = jnp.where(vis, wb, 0.0)
            w.append(wb)
            run = run + r[:, PAGE:]
        run_ref[...] = run
        for b0 in range(0, nb, group):
            for g in range(n_pairs):
                sl = slice(2 * g * n_new, (2 * g + 2) * n_new)
                wg = jnp.concatenate([w[b][sl] for b in range(b0, b0 + group)], axis=1).astype(BF16)
                vg = v_blocks[b0][g] if group == 1 else jnp.concatenate(
                    [v_blocks[b][g] for b in range(b0, b0 + group)], axis=0)
                r = _dot(wg, vg)
                acc_ref[2 * g * n_new:(2 * g + 1) * n_new, :] += r[:n_new, :HEAD]
                acc_ref[(2 * g + 1) * n_new:(2 * g + 2) * n_new, :] += r[n_new:, HEAD:]

    @pl.when(j == 0)
    def _():
        acc_ref[...] = jnp.zeros_like(acc_ref)
        run_ref[...] = jnp.zeros_like(run_ref)
        pad = jnp.zeros((PAGE - n_new, 2 * HEAD), F32)
        pair = lambda ref, g: jnp.concatenate([ref[:, 2 * g * HEAD:(2 * g + 2) * HEAD], pad], axis=0).astype(BF16)
        s_idx = lax.broadcasted_iota(jnp.int32, (rows, PAGE), 1)
        i_idx = lax.broadcasted_iota(jnp.int32, (rows, PAGE), 0) % n_new
        attend([[pair(kn_ref, g) for g in range(n_pairs)]], [[pair(vn_ref, g) for g in range(n_pairs)]], s_idx < i_idx)

    def head_pair(ref, g):
        return jnp.concatenate([ref[pl.ds(2 * g, PAGE, stride=N_C), :], ref[pl.ds(2 * g + 1, PAGE, stride=N_C), :]],
                               axis=1).astype(BF16)

    attend([[head_pair(k_refs[p], g) for g in range(n_pairs)] for p in range(npg)],
           [[head_pair(v_refs[p], g) for g in range(n_pairs)] for p in range(npg)], None)

    @pl.when(j == pl.num_programs(1) - 1)
    def _():
        for h in range(N_C):
            sl = slice(h * HEAD, (h + 1) * HEAD)
            o_ref[:, sl] = _rms_rows(acc_ref[h * n_new:(h + 1) * n_new, :]) * on_ref[:, sl]


def _sb_sample(proj, cache_k, cache_v, layer, page_idx, bias_rows, onorm, n_batch, n_new, n_pages):
    npg = PAGES_PER_STEP
    steps = n_pages // npg
    width = N_C * HEAD
    rows = N_C * n_new
    idx = np.arange(PAGE)
    tri = jnp.asarray(np.concatenate([idx[:, None] >= idx[None, :], np.ones((PAGE, PAGE), bool)], axis=1), BF16)

    def page_spec(p):
        return pl.BlockSpec((None, None, PAGE * N_C, HEAD),
                            lambda b, j, pt: (layer, pt[b * n_pages + n_pages - 1 - (j * npg + p)], 0, 0))

    def const(shape):
        return pl.BlockSpec(shape, lambda b, j, pt: (0, 0))

    grid_spec = pltpu.PrefetchScalarGridSpec(
        num_scalar_prefetch=1,
        grid=(n_batch, steps),
        in_specs=[
            pl.BlockSpec((n_new, width), lambda b, j, pt: (b, COL_CQ // N_C)),
            pl.BlockSpec((n_new, width), lambda b, j, pt: (b, COL_CK // N_C)),
            pl.BlockSpec((n_new, width), lambda b, j, pt: (b, COL_CV // N_C)),
            const((rows, PAGE)), const((1, width)), const((PAGE, 2 * PAGE)),
        ] + [page_spec(p) for p in range(npg)] * 2,
        out_specs=pl.BlockSpec((n_new, width), lambda b, j, pt: (b, 0)),
        scratch_shapes=[pltpu.VMEM((rows, HEAD), F32), pltpu.VMEM((rows, PAGE), F32)],
    )
    kern = functools.partial(_sb_sample_kernel, n_new=n_new, scale=HEAD ** -0.5)
    return pl.pallas_call(
        kern,
        grid_spec=grid_spec,
        out_shape=jax.ShapeDtypeStruct((n_batch * n_new, width), F32),
        compiler_params=_params(("parallel", "arbitrary")),
        name="sb_sample",
    )(page_idx, proj, proj, proj, bias_rows, onorm, tri, *([cache_k] * npg), *([cache_v] * npg))


def _prep_w_in(w):
    w = w.astype(BF16)
    a, b_q, b_k, b_v, b_g, b_lr, c = (w[..., :2048], w[..., 2048:2304], w[..., 2304:2560], w[..., 2560:3072],
                                      w[..., 3072:3584], w[..., 3584:3600], w[..., 3600:])
    zq = jnp.zeros(w.shape[:-1] + (HEAD - GLA_DK,), w.dtype)
    bq_parts, bk_parts = [], []
    for h in range(N_B):
        sl = slice(h * GLA_DK, (h + 1) * GLA_DK)
        pad = jnp.concatenate([b_lr, zq[..., b_lr.shape[-1]:]], axis=-1) if h == 0 else zq
        bq_parts += [b_q[..., sl], pad]
        bk_parts += [b_k[..., sl], zq]
    out = jnp.concatenate([c, a] + bq_parts + bk_parts + [b_v, b_g], axis=-1)
    assert out.shape[-1] == PROJ_W
    return out


def _prep_gla_gate(w_up, b_up):
    rank = w_up.shape[0]
    wp = jnp.zeros((HEAD, N_B * HEAD), F32)
    bp = jnp.zeros((1, N_B * HEAD), F32)
    for h in range(N_B):
        wp = wp.at[LR_LANE0:LR_LANE0 + rank, h * HEAD:h * HEAD + GLA_DK].set(w_up[:, h * GLA_DK:(h + 1) * GLA_DK])
        bp = bp.at[0, h * HEAD:h * HEAD + GLA_DK].set(b_up[h * GLA_DK:(h + 1) * GLA_DK])
    return wp.astype(BF16), bp


def _pad_ff(w, axis):
    ff = w.shape[axis]
    ffp = -(-ff // FF_TILE) * FF_TILE
    pad = [(0, 0)] * w.ndim
    pad[axis] = (0, ffp - ff)
    return jnp.pad(w.astype(BF16), pad)


def kernel(x_prompt, x_sample, state_hgrn, state_gla, cache_k, cache_v, page_table, norm_ffn1, w_ffn1_gate, w_ffn1_up, w_ffn1_down, norm_mix, w_in, w_gla_alpha_up, b_gla_alpha, hgrn_lb_logits, qnorm_c, knorm_c, sb_bias, onorm_a, onorm_b, onorm_c, w_out, norm_ffn2, w_ffn2_gate, w_ffn2_up, w_ffn2_down):
    depth = w_in.shape[0]
    bp, tp, d = x_prompt.shape
    bs, ts, _ = x_sample.shape
    n_pages = page_table.shape[1]

    lb_cum = jnp.cumsum(jax.nn.softmax(hgrn_lb_logits.astype(F32), axis=0), axis=0)
    lb_all = lb_cum - lb_cum[0:1]

    xp = x_prompt.reshape(bp * tp, d)
    xs = x_sample.reshape(bs * ts, d)
    pt_flat = page_table.reshape(-1).astype(jnp.int32)
    ck = cache_k.reshape(cache_k.shape[:2] + (PAGE * N_C, HEAD))
    cv = cache_v.reshape(cache_v.shape[:2] + (PAGE * N_C, HEAD))

    ffn1_w = (_pad_ff(w_ffn1_gate, 2), _pad_ff(w_ffn1_up, 2), _pad_ff(w_ffn1_down, 1))
    ffn2_w = (_pad_ff(w_ffn2_gate, 2), _pad_ff(w_ffn2_up, 2), _pad_ff(w_ffn2_down, 1))
    w_in_b = _prep_w_in(w_in)
    w_out_b = w_out.astype(BF16)
    consts_p = _rec_consts(REC_CHUNK, REC_CHUNK)
    consts_s = _rec_consts(SAMPLE_GROUP * ts, ts)

    outs = {k: [] for k in ("hp", "gp", "kp", "vp", "hs", "gs", "ks", "vs")}
    for l in range(depth):
        row = lambda a: a[l].reshape(1, -1).astype(F32)
        ffn1 = (row(norm_ffn1),) + ffn1_w + (l,)
        ffn2 = (row(norm_ffn2),) + ffn2_w + (l,)
        lb = lb_all[l]
        lb3 = jnp.stack([jnp.log(lb), jnp.log1p(-lb), 1.0 - lb])
        hgrn_consts = (lb3, row(onorm_a))
        gla_consts = _prep_gla_gate(w_gla_alpha_up[l].astype(F32), b_gla_alpha[l].astype(F32)) + (row(onorm_b),)
        bias = sb_bias[l].astype(F32)
        on_c = row(onorm_c)

        xp = _ffn(xp, *ffn1)
        pj, kn, vn = _proj(xp, row(norm_mix), w_in_b, row(qnorm_c), row(knorm_c), l)
        rec = dict(chunk=REC_CHUNK, seq=REC_CHUNK, n_chunks=REC_ROWS // REC_CHUNK, seq_steps=tp // REC_ROWS, out_dtype=BF16)
        oa, hp = _recurrence("hgrn", pj, bp, None, consts_p, hgrn_consts, **rec)
        ob, gp = _recurrence("gla", pj, bp, None, consts_p, gla_consts, **rec)
        oc = _sb_prompt(pj, bias, on_c, bp, tp)
        xp = _outproj(xp, oa, ob, oc, w_out_b, l)
        xp = _ffn(xp, *ffn2)
        outs["hp"].append(hp)
        outs["gp"].append(gp)
        outs["kp"].append(kn.reshape(bp, tp, N_C, HEAD))
        outs["vp"].append(vn.reshape(bp, tp, N_C, HEAD))

        xs = _ffn(xs, *ffn1)
        pj, kn, vn = _proj(xs, row(norm_mix), w_in_b, row(qnorm_c), row(knorm_c), l)
        rec = dict(chunk=SAMPLE_GROUP * ts, seq=ts, n_chunks=1, seq_steps=1, out_dtype=BF16)
        oa, hs = _recurrence("hgrn", pj, bs // SAMPLE_GROUP, state_hgrn[l], consts_s, hgrn_consts, **rec)
        ob, gs = _recurrence("gla", pj, bs // SAMPLE_GROUP, state_gla[l], consts_s, gla_consts, **rec)
        bias_rows = jnp.broadcast_to(jnp.repeat(bias, ts)[:, None], (N_C * ts, PAGE))
        oc = _sb_sample(pj, ck, cv, l, pt_flat, bias_rows, on_c, bs, ts, n_pages)
        xs = _outproj(xs, oa, ob, oc, w_out_b, l)
        xs = _ffn(xs, *ffn2)
        outs["hs"].append(hs)
        outs["gs"].append(gs)
        outs["ks"].append(kn.reshape(bs, ts, N_C, HEAD))
        outs["vs"].append(vn.reshape(bs, ts, N_C, HEAD))

    st = lambda k: jnp.stack(outs[k])
    return (xp.reshape(bp, tp, d), xs.reshape(bs, ts, d), st("hp"), st("gp"), st("kp"), st("vp"),
            st("hs"), st("gs"), st("ks"), st("vs"))
```
